```python
import math
import jax, jax.numpy as jnp
from jax import lax
import numpy as np

D_MODEL = 1024
BATCH = 8
SEQ = 2048
DEPTH = 2

HEAD_DIM = 64
N_Q_HEADS = 8
N_KV_HEADS = 2
WINDOW = 128
BLOCK = 128
N_BUCKETS = 32
MAX_DISTANCE = 128
SSM_HEADS = 8
SSM_HEAD_DIM = 64
SSM_GROUPS = 2
SSM_STATE = 128
CONV_WIDTH = 4
CHUNK = 128
D_FF = 4 * D_MODEL

D_ATTN = N_Q_HEADS * HEAD_DIM
D_KV = N_KV_HEADS * HEAD_DIM
D_SSM = SSM_HEADS * SSM_HEAD_DIM
D_BC = SSM_GROUPS * SSM_STATE
D_CONV = D_SSM + 2 * D_BC
D_MIX = D_ATTN + D_SSM
D_IN = D_ATTN + 2 * D_KV + D_SSM + D_CONV + SSM_HEADS
SPLITS = [D_ATTN, D_ATTN + D_KV, D_ATTN + 2 * D_KV, D_ATTN + 2 * D_KV + D_SSM,
          D_ATTN + 2 * D_KV + D_SSM + D_CONV]
EPS = 1e-6

kernel_name = "hymba_swa_sink_ssd_hybrid"


def rms_norm(x, g):
    xf = x.astype(jnp.float32)
    y = xf * lax.rsqrt(jnp.mean(jnp.square(xf), axis=-1, keepdims=True) + EPS)
    return (y * g.astype(jnp.float32)).astype(x.dtype)


def t5_causal_bucket(dist):
    max_exact = N_BUCKETS // 2
    d_f = jnp.maximum(dist, 1).astype(jnp.float32)
    large = max_exact + (jnp.log(d_f / max_exact) / math.log(MAX_DISTANCE / max_exact)
                         * (N_BUCKETS - max_exact)).astype(jnp.int32)
    large = jnp.minimum(large, N_BUCKETS - 1)
    return jnp.where(dist < max_exact, dist, large)


def band_bias_and_mask(rel_bias, n_blocks):
    qi = jnp.arange(BLOCK)[:, None]
    kj = jnp.arange(2 * BLOCK)[None, :]
    dist = qi + BLOCK - kj
    in_window = (dist >= 0) & (dist < WINDOW)
    bucket = t5_causal_bucket(jnp.clip(dist, 0, None))
    bias = jnp.transpose(rel_bias[bucket], (2, 0, 1))
    key_pos = jnp.arange(n_blocks)[:, None] * BLOCK + kj - BLOCK
    mask = in_window[None] & (key_pos >= 0)[:, None, :]
    return bias, mask


def sliding_window_attention(q, k, v, q_gain, k_gain, sinks, bias, mask):
    b, s = q.shape[:2]
    nb = s // BLOCK
    g = N_Q_HEADS // N_KV_HEADS
    q = rms_norm(q.reshape(b, s, N_Q_HEADS, HEAD_DIM), q_gain)
    k = rms_norm(k.reshape(b, s, N_KV_HEADS, HEAD_DIM), k_gain)
    v = v.reshape(b, s, N_KV_HEADS, HEAD_DIM)
    qb = q.reshape(b, nb, BLOCK, N_KV_HEADS, g, HEAD_DIM)

    def band(t):
        t = t.reshape(b, nb, BLOCK, N_KV_HEADS, HEAD_DIM)
        prev = jnp.pad(t, ((0, 0), (1, 0), (0, 0), (0, 0), (0, 0)))[:, :-1]
        return jnp.concatenate([prev, t], axis=2)

    kb, vb = band(k), band(v)
    scores = jnp.einsum('bnqhgd,bnkhd->bnhgqk', qb, kb).astype(jnp.float32) * (HEAD_DIM ** -0.5)
    scores = scores + bias.reshape(N_KV_HEADS, g, BLOCK, 2 * BLOCK).astype(jnp.float32)
    scores = jnp.where(mask[None, :, None, None], scores, -jnp.inf)
    sink = jnp.broadcast_to(sinks.reshape(N_KV_HEADS, g, 1, 1).astype(jnp.float32),
                            scores.shape[:-1] + (1,))
    probs = jax.nn.softmax(jnp.concatenate([scores, sink], axis=-1), axis=-1)[..., :-1]
    out = jnp.einsum('bnhgqk,bnkhd->bnqhgd', probs.astype(v.dtype), vb)
    return out.reshape(b, s, D_ATTN)


def causal_depthwise_conv(u, w, bias):
    out = lax.conv_general_dilated(u, w[:, None, :], window_strides=(1,),
                                   padding=[(CONV_WIDTH - 1, 0)],
                                   dimension_numbers=('NWC', 'WIO', 'NWC'),
                                   feature_group_count=u.shape[-1])
    return out + bias


def ssd_mixer(z, xbc, dt_raw, conv_w, conv_b, dt_bias, a_log, d_skip, norm_g):
    f32 = jnp.float32
    b, s = z.shape[:2]
    nc = s // CHUNK
    r = SSM_HEADS // SSM_GROUPS
    xbc = jax.nn.silu(causal_depthwise_conv(xbc, conv_w, conv_b))
    xs, bm, cm = jnp.split(xbc, [D_SSM, D_SSM + D_BC], axis=-1)
    xs = xs.astype(f32).reshape(b, nc, CHUNK, SSM_GROUPS, r, SSM_HEAD_DIM)
    bm = bm.astype(f32).reshape(b, nc, CHUNK, SSM_GROUPS, SSM_STATE)
    cm = cm.astype(f32).reshape(b, nc, CHUNK, SSM_GROUPS, SSM_STATE)
    dt = jax.nn.softplus(dt_raw.astype(f32) + dt_bias.astype(f32)).reshape(b, nc, CHUNK, SSM_GROUPS, r)
    a = -jnp.exp(a_log.astype(f32)).reshape(SSM_GROUPS, r)
    a_cs = jnp.cumsum(dt * a, axis=2)
    xdt = xs * dt[..., None]
    li = jnp.arange(CHUNK)
    causal = (li[:, None] >= li[None, :])[:, :, None, None]
    seg = a_cs[:, :, :, None] - a_cs[:, :, None, :]
    decay = jnp.exp(jnp.where(causal, seg, -jnp.inf))
    cb = jnp.einsum('bclgn,bcsgn->bclsg', cm, bm)
    y_diag = jnp.einsum('bclsgr,bcsgrp->bclgrp', cb[..., None] * decay, xdt)
    decay_to_end = jnp.exp(a_cs[:, :, -1:] - a_cs)
    states = jnp.einsum('bclgn,bclgr,bclgrp->bcgrpn', bm, decay_to_end, xdt)
    chunk_decay = jnp.exp(a_cs[:, :, -1])

    def step(h, inp):
        st, dec = inp
        return h * dec[..., None, None] + st, h

    h0 = jnp.zeros((b, SSM_GROUPS, r, SSM_HEAD_DIM, SSM_STATE), f32)
    _, prev = lax.scan(step, h0, (jnp.moveaxis(states, 1, 0), jnp.moveaxis(chunk_decay, 1, 0)))
    prev = jnp.moveaxis(prev, 0, 1)
    y_off = jnp.einsum('bclgn,bcgrpn,bclgr->bclgrp', cm, prev, jnp.exp(a_cs))
    y = y_diag + y_off + xs * d_skip.astype(f32).reshape(SSM_GROUPS, r)[:, :, None]
    y = y.reshape(b, s, D_SSM) * jax.nn.silu(z.astype(f32))
    yg = y.reshape(b, s, SSM_GROUPS, D_SSM // SSM_GROUPS)
    yg = yg * lax.rsqrt(jnp.mean(jnp.square(yg), axis=-1, keepdims=True) + EPS)
    y = yg.reshape(b, s, D_SSM) * norm_g.astype(f32)
    return y.astype(z.dtype)


def setup_inputs(seed: int = 0) -> dict:
    key = jax.random.key(seed)
    ks = jax.random.split(key, 20)
    nrm = jax.random.normal
    dt0 = jnp.exp(jax.random.uniform(ks[9], (DEPTH, SSM_HEADS), minval=math.log(1e-3), maxval=math.log(1e-1)))
    return {
        "x": nrm(ks[0], (BATCH, SEQ, D_MODEL), jnp.float32),
        "mix_norm_g": 1.0 + 0.01 * nrm(ks[1], (DEPTH, D_MODEL), jnp.float32),
        "w_in": nrm(ks[2], (DEPTH, D_MODEL, D_IN), jnp.float32) * D_MODEL ** -0.5,
        "q_gain": 1.0 + 0.01 * nrm(ks[3], (DEPTH, HEAD_DIM), jnp.float32),
        "k_gain": 1.0 + 0.01 * nrm(ks[4], (DEPTH, HEAD_DIM), jnp.float32),
        "sinks": 0.5 * nrm(ks[5], (DEPTH, N_Q_HEADS), jnp.float32),
        "rel_bias": 0.1 * nrm(ks[6], (N_BUCKETS, N_Q_HEADS), jnp.float32),
        "conv_w": nrm(ks[7], (DEPTH, CONV_WIDTH, D_CONV), jnp.float32) * CONV_WIDTH ** -0.5,
        "conv_b": 0.01 * nrm(ks[8], (DEPTH, D_CONV), jnp.float32),
        "dt_bias": dt0 + jnp.log(-jnp.expm1(-dt0)),
        "a_log": jnp.log(jax.random.uniform(ks[10], (DEPTH, SSM_HEADS), minval=1.0, maxval=16.0)),
        "d_skip": 1.0 + 0.01 * nrm(ks[11], (DEPTH, SSM_HEADS), jnp.float32),
        "ssm_norm_g": 1.0 + 0.01 * nrm(ks[12], (DEPTH, D_SSM), jnp.float32),
        "w_out": nrm(ks[13], (DEPTH, D_MIX, D_MODEL), jnp.float32) * D_MIX ** -0.5,
        "mlp_norm_g": 1.0 + 0.01 * nrm(ks[14], (DEPTH, D_MODEL), jnp.float32),
        "w_up": nrm(ks[15], (DEPTH, D_MODEL, D_FF), jnp.float32) * D_MODEL ** -0.5,
        "w_down": nrm(ks[16], (DEPTH, D_FF, D_MODEL), jnp.float32) * D_FF ** -0.5,
    }


def reference(x, mix_norm_g, w_in, q_gain, k_gain, sinks, rel_bias, conv_w, conv_b,
              dt_bias, a_log, d_skip, ssm_norm_g, w_out, mlp_norm_g, w_up, w_down):
    bias, mask = band_bias_and_mask(rel_bias, x.shape[1] // BLOCK)
    for l in range(DEPTH):
        h = rms_norm(x, mix_norm_g[l])
        proj = h @ w_in[l]
        q, k, v, z, xbc, dt_raw = jnp.split(proj, SPLITS, axis=-1)
        attn = sliding_window_attention(q, k, v, q_gain[l], k_gain[l], sinks[l], bias, mask)
        ssm = ssd_mixer(z, xbc, dt_raw, conv_w[l], conv_b[l], dt_bias[l], a_log[l],
                        d_skip[l], ssm_norm_g[l])
        x = x + jnp.concatenate([attn, ssm], axis=-1) @ w_out[l]
        h = rms_norm(x, mlp_norm_g[l])
        x = x + jnp.square(jax.nn.relu(h @ w_up[l])) @ w_down[l]
    return x
```

```python
import functools
import math

import numpy as np
import jax
import jax.numpy as jnp
from jax import lax
from jax.experimental import pallas as pl
from jax.experimental.pallas import tpu as pltpu

D_MODEL = 1024
HEAD_DIM = 64
N_Q_HEADS = 8
N_KV_HEADS = 2
Q_PER_KV = N_Q_HEADS // N_KV_HEADS
WINDOW = 128
BLOCK = 128
N_BUCKETS = 32
MAX_DISTANCE = 128
SSM_HEADS = 8
SSM_HEAD_DIM = 64
SSM_GROUPS = 2
HEADS_PER_GROUP = SSM_HEADS // SSM_GROUPS
SSM_STATE = 128
CONV_WIDTH = 4
CHUNK = 128
D_FF = 4 * D_MODEL
D_ATTN = N_Q_HEADS * HEAD_DIM
D_KV = N_KV_HEADS * HEAD_DIM
D_SSM = SSM_HEADS * SSM_HEAD_DIM
D_BC = SSM_GROUPS * SSM_STATE
D_CONV = D_SSM + 2 * D_BC
D_GROUP = D_SSM // SSM_GROUPS
EPS = 1e-6

LANES = 128
SUBLANES = 8
DT_PAD = LANES
COL_Q = 0
COL_KV = D_ATTN
COL_Z = COL_KV + 2 * D_KV
COL_XBC = COL_Z + D_SSM
COL_DT = COL_XBC + D_CONV
D_IN_PAD = COL_DT + DT_PAD

TOKEN_TILE = 512
FF_TILE = 1024
VMEM_LIMIT = 56 * 1024 * 1024

F32 = jnp.float32
BF16 = jnp.bfloat16


def _rms_scale(x):
    return x * lax.rsqrt(jnp.mean(jnp.square(x), axis=-1, keepdims=True) + EPS)


def _resident(shape):
    nd = len(shape)
    return pl.BlockSpec(shape, lambda *_: (0,) * nd, pipeline_mode=pl.Buffered(1))


def _band_tables():
    qi = np.arange(BLOCK)[:, None]
    kj = np.arange(2 * BLOCK)[None, :]
    dist = qi + BLOCK - kj
    in_window = (dist >= 0) & (dist < WINDOW)
    d = np.clip(dist, 0, None)
    max_exact = N_BUCKETS // 2
    d_f = np.maximum(d, 1).astype(np.float32)
    large = max_exact + (np.log(d_f / np.float32(max_exact)) / np.float32(math.log(MAX_DISTANCE / max_exact))
                         * np.float32(N_BUCKETS - max_exact)).astype(np.int32)
    large = np.minimum(large, N_BUCKETS - 1)
    bucket = np.where(d < max_exact, d, large).astype(np.int32)
    bucket = np.where(in_window, bucket, -1).astype(np.int32)
    return bucket


def _bias_table_kernel(rel_ref, bucket_ref, out_ref):
    bucket = bucket_ref[...]
    for h in range(N_Q_HEADS):
        acc = jnp.full(bucket.shape, -jnp.inf, F32)
        for b in range(N_BUCKETS):
            acc = jnp.where(bucket == b, rel_ref[b, h], acc)
        out_ref[h] = acc


def _bias_table(rel_bias):
    bucket = jnp.asarray(_band_tables())
    return pl.pallas_call(
        _bias_table_kernel,
        out_shape=jax.ShapeDtypeStruct((N_Q_HEADS, BLOCK, 2 * BLOCK), F32),
        in_specs=[pl.BlockSpec(memory_space=pltpu.SMEM),
                  pl.BlockSpec(memory_space=pltpu.VMEM)],
        out_specs=pl.BlockSpec(memory_space=pltpu.VMEM),
        name="bias_table",
    )(rel_bias, bucket)


def _in_proj_kernel(x_ref, g_ref, w_ref, q_ref, kv_ref, z_ref, xbc_ref, dt_ref):
    h = (_rms_scale(x_ref[...]) * g_ref[...]).astype(BF16)
    proj = jnp.dot(h, w_ref[...], preferred_element_type=F32)
    q_ref[...] = proj[:, COL_Q:COL_KV].astype(BF16)
    kv_ref[...] = proj[:, COL_KV:COL_Z].astype(BF16)
    z_ref[...] = proj[:, COL_Z:COL_XBC].astype(BF16)
    xbc_ref[...] = proj[:, COL_XBC:COL_DT].astype(BF16)
    dt_ref[...] = proj[:, COL_DT:D_IN_PAD]


def _in_proj(x2, g, w_pad):
    t = x2.shape[0]
    row = lambda i: (i, 0)
    widths = (D_ATTN, 2 * D_KV, D_SSM, D_CONV)
    return pl.pallas_call(
        _in_proj_kernel,
        grid=(t // TOKEN_TILE,),
        in_specs=[pl.BlockSpec((TOKEN_TILE, D_MODEL), row),
                  _resident((1, D_MODEL)),
                  _resident((D_MODEL, D_IN_PAD))],
        out_specs=[pl.BlockSpec((TOKEN_TILE, w), row) for w in widths]
                  + [pl.BlockSpec((TOKEN_TILE, DT_PAD), row)],
        out_shape=[jax.ShapeDtypeStruct((t, w), BF16) for w in widths]
                  + [jax.ShapeDtypeStruct((t, DT_PAD), F32)],
        compiler_params=pltpu.CompilerParams(
            dimension_semantics=("arbitrary",), vmem_limit_bytes=VMEM_LIMIT),
        name="in_proj",
    )(x2, g, w_pad)


def _attention_kernel(sink_ref, q_ref, kvp_ref, kvc_ref, bias_ref, qg_ref, kg_ref, o_ref):
    n = pl.program_id(1)
    q = q_ref[...].astype(F32)
    kv = jnp.concatenate([kvp_ref[...], kvc_ref[...]], axis=0)
    col = lax.broadcasted_iota(jnp.int32, (BLOCK, 2 * BLOCK), 1)
    key_exists = (col >= BLOCK) | (n > 0)
    qg = qg_ref[...] * (HEAD_DIM ** -0.5)
    kg = kg_ref[...]
    outs = []
    for kvh in range(N_KV_HEADS):
        k = kv[:, kvh * HEAD_DIM:(kvh + 1) * HEAD_DIM].astype(F32)
        k = (_rms_scale(k) * kg).astype(BF16)
        v = kv[:, D_KV + kvh * HEAD_DIM:D_KV + (kvh + 1) * HEAD_DIM]
        for gi in range(Q_PER_KV):
            h = kvh * Q_PER_KV + gi
            qh = q[:, h * HEAD_DIM:(h + 1) * HEAD_DIM]
            qh = (_rms_scale(qh) * qg).astype(BF16)
            s = lax.dot_general(qh, k, (((1,), (1,)), ((), ())), preferred_element_type=F32)
            s = jnp.where(key_exists, s + bias_ref[h], -jnp.inf)
            sink = sink_ref[h]
            m = jnp.maximum(jnp.max(s, axis=-1, keepdims=True), sink)
            p = jnp.exp(s - m)
            denom = jnp.sum(p, axis=-1, keepdims=True) + jnp.exp(sink - m)
            o = jnp.dot(p.astype(BF16), v, preferred_element_type=F32)
            outs.append(o / denom)
    o_ref[...] = jnp.concatenate(outs, axis=-1).astype(BF16)


def _attention(q, kv, bias, q_gain, k_gain, sinks, batch, seq):
    nb = seq // BLOCK
    cur = lambda b, n: (b * nb + n, 0)
    prev = lambda b, n: (b * nb + jnp.maximum(n - 1, 0), 0)
    return pl.pallas_call(
        _attention_kernel,
        grid=(batch, nb),
        in_specs=[pl.BlockSpec(memory_space=pltpu.SMEM),
                  pl.BlockSpec((BLOCK, D_ATTN), cur),
                  pl.BlockSpec((BLOCK, 2 * D_KV), prev),
                  pl.BlockSpec((BLOCK, 2 * D_KV), cur),
                  _resident((N_Q_HEADS, BLOCK, 2 * BLOCK)),
                  _resident((1, HEAD_DIM)),
                  _resident((1, HEAD_DIM))],
        out_specs=pl.BlockSpec((BLOCK, D_ATTN), cur),
        out_shape=jax.ShapeDtypeStruct((batch * seq, D_ATTN), BF16),
        compiler_params=pltpu.CompilerParams(
            dimension_semantics=("arbitrary", "arbitrary"), vmem_limit_bytes=VMEM_LIMIT),
        name="attention",
    )(sinks, q, kv, kv, bias, q_gain, k_gain)


def _split3(x):
    hi = x.astype(BF16)
    r = x - hi.astype(F32)
    mid = r.astype(BF16)
    lo = (r - mid.astype(F32)).astype(BF16)
    return hi, mid, lo


def _expand_heads(xcol):
    lane = lax.broadcasted_iota(jnp.int32, (CHUNK, LANES), 1)
    parts = []
    for j in range(SSM_HEADS // 2):
        lo = jnp.broadcast_to(xcol[:, 2 * j:2 * j + 1], (CHUNK, LANES))
        hi = jnp.broadcast_to(xcol[:, 2 * j + 1:2 * j + 2], (CHUNK, LANES))
        parts.append(jnp.where(lane < SSM_HEAD_DIM, lo, hi))
    return jnp.concatenate(parts, axis=-1)


def _ssd_kernel(z_ref, xbc_ref, dt_ref, cw_ref, cb_ref, dtb_ref, alog_ref, dsk_ref, ng_ref,
                y_ref, ext_ref, state_ref):
    c = pl.program_id(1)

    @pl.when(c == 0)
    def _():
        ext_ref[0:SUBLANES, :] = jnp.zeros((SUBLANES, D_CONV), F32)
        state_ref[...] = jnp.zeros(state_ref.shape, F32)

    ext_ref[SUBLANES:SUBLANES + CHUNK, :] = xbc_ref[...].astype(F32)
    conv = jnp.broadcast_to(cb_ref[...], (CHUNK, D_CONV))
    for k in range(CONV_WIDTH):
        start = SUBLANES - (CONV_WIDTH - 1) + k
        conv = conv + cw_ref[k:k + 1, :] * ext_ref[start:start + CHUNK, :]
    ext_ref[0:SUBLANES, :] = ext_ref[CHUNK:CHUNK + SUBLANES, :]
    xbc = conv * jax.nn.sigmoid(conv)
    xs = xbc[:, :D_SSM]

    dt = jax.nn.softplus(dt_ref[...] + dtb_ref[...])
    d_a = dt * -jnp.exp(alog_ref[...])
    row = lax.broadcasted_iota(jnp.int32, (CHUNK, CHUNK), 0)
    colm = lax.broadcasted_iota(jnp.int32, (CHUNK, CHUNK), 1)
    causal = row >= colm
    tril = causal.astype(BF16)
    a_cs = sum(jnp.dot(tril, piece, preferred_element_type=F32) for piece in _split3(d_a))
    a_cs_t = a_cs.T
    a_last = a_cs[CHUNK - 1:CHUNK, :]

    dt_x = _expand_heads(dt)
    a_cs_x = _expand_heads(a_cs)
    a_last_x = a_cs_x[CHUNK - 1:CHUNK, :]
    xdt = xs * dt_x
    x_end = (xdt * jnp.exp(a_last_x - a_cs_x)).astype(BF16)
    chunk_decay = jnp.exp(a_last_x)
    from_start = jnp.exp(a_cs_x)
    xdt_b = xdt.astype(BF16)

    ys = []
    for g in range(SSM_GROUPS):
        bm = xbc[:, D_SSM + g * SSM_STATE:D_SSM + (g + 1) * SSM_STATE]
        cm = xbc[:, D_SSM + D_BC + g * SSM_STATE:D_SSM + D_BC + (g + 1) * SSM_STATE].astype(BF16)
        bm_b = bm.astype(BF16)
        cb = lax.dot_general(cm, bm_b, (((1,), (1,)), ((), ())), preferred_element_type=F32)
        gs = slice(g * D_GROUP, (g + 1) * D_GROUP)
        prev = state_ref[g]
        y_off = jnp.dot(cm, prev.astype(BF16), preferred_element_type=F32) * from_start[:, gs]
        new = jnp.dot(bm.T.astype(BF16), x_end[:, gs], preferred_element_type=F32)
        state_ref[g] = prev * chunk_decay[:, gs] + new
        y_heads = []
        for r in range(HEADS_PER_GROUP):
            h = g * HEADS_PER_GROUP + r
            seg = a_cs[:, h:h + 1] - a_cs_t[h:h + 1, :]
            decay = jnp.exp(jnp.where(causal, seg, -jnp.inf))
            hs = slice(h * SSM_HEAD_DIM, (h + 1) * SSM_HEAD_DIM)
            y_heads.append(jnp.dot((cb * decay).astype(BF16), xdt_b[:, hs], preferred_element_type=F32))
        ys.append(jnp.concatenate(y_heads, axis=-1) + y_off)
    y = jnp.concatenate(ys, axis=-1) + xs * dsk_ref[...]
    z = z_ref[...].astype(F32)
    y = y * (z * jax.nn.sigmoid(z))
    y = jnp.concatenate([_rms_scale(y[:, g * D_GROUP:(g + 1) * D_GROUP]) for g in range(SSM_GROUPS)], axis=-1)
    y_ref[...] = (y * ng_ref[...]).astype(BF16)


def _ssd(z, xbc, dt, conv_w, conv_b, dt_bias, a_log, d_skip, norm_g, batch, seq):
    nc = seq // CHUNK
    cur = lambda b, c: (b * nc + c, 0)
    return pl.pallas_call(
        _ssd_kernel,
        grid=(batch, nc),
        in_specs=[pl.BlockSpec((CHUNK, D_SSM), cur),
                  pl.BlockSpec((CHUNK, D_CONV), cur),
                  pl.BlockSpec((CHUNK, DT_PAD), cur),
                  _resident((CONV_WIDTH, D_CONV)),
                  _resident((1, D_CONV)),
                  _resident((1, DT_PAD)),
                  _resident((1, DT_PAD)),
                  _resident((1, D_SSM)),
                  _resident((1, D_SSM))],
        out_specs=pl.BlockSpec((CHUNK, D_SSM), cur),
        out_shape=jax.ShapeDtypeStruct((batch * seq, D_SSM), BF16),
        scratch_shapes=[pltpu.VMEM((CHUNK + SUBLANES, D_CONV), F32),
                        pltpu.VMEM((SSM_GROUPS, SSM_STATE, D_GROUP), F32)],
        compiler_params=pltpu.CompilerParams(
            dimension_semantics=("arbitrary", "arbitrary"), vmem_limit_bytes=VMEM_LIMIT),
        name="ssd",
    )(z, xbc, dt, conv_w, conv_b, dt_bias, a_log, d_skip, norm_g)


def _out_mlp_kernel(x_ref, attn_ref, ssm_ref, wo_ref, g_ref, wu_ref, wd_ref, o_ref):
    mix = jnp.dot(attn_ref[...], wo_ref[:D_ATTN, :], preferred_element_type=F32)
    mix = mix + jnp.dot(ssm_ref[...], wo_ref[D_ATTN:, :], preferred_element_type=F32)
    x1 = x_ref[...] + mix
    h = (_rms_scale(x1) * g_ref[...]).astype(BF16)
    acc = x1
    for j in range(D_FF // FF_TILE):
        u = jnp.dot(h, wu_ref[:, j * FF_TILE:(j + 1) * FF_TILE], preferred_element_type=F32)
        u = jnp.square(jnp.maximum(u, 0.0)).astype(BF16)
        acc = acc + jnp.dot(u, wd_ref[j * FF_TILE:(j + 1) * FF_TILE, :], preferred_element_type=F32)
    o_ref[...] = acc


def _out_mlp(x2, attn, ssm, w_out, g, w_up, w_down):
    t = x2.shape[0]
    row = lambda i: (i, 0)
    return pl.pallas_call(
        _out_mlp_kernel,
        grid=(t // TOKEN_TILE,),
        in_specs=[pl.BlockSpec((TOKEN_TILE, D_MODEL), row),
                  pl.BlockSpec((TOKEN_TILE, D_ATTN), row),
                  pl.BlockSpec((TOKEN_TILE, D_SSM), row),
                  _resident((D_ATTN + D_SSM, D_MODEL)),
                  _resident((1, D_MODEL)),
                  _resident((D_MODEL, D_FF)),
                  _resident((D_FF, D_MODEL))],
        out_specs=pl.BlockSpec((TOKEN_TILE, D_MODEL), row),
        out_shape=jax.ShapeDtypeStruct((t, D_MODEL), F32),
        compiler_params=pltpu.CompilerParams(
            dimension_semantics=("arbitrary",), vmem_limit_bytes=VMEM_LIMIT),
        name="out_mlp",
    )(x2, attn, ssm, w_out, g, w_up, w_down)


def _pad_lanes(v, width):
    return jnp.pad(v, (0, width - v.shape[0]))[None, :]


def kernel(x, mix_norm_g, w_in, q_gain, k_gain, sinks, rel_bias, conv_w, conv_b, dt_bias, a_log,
           d_skip, ssm_norm_g, w_out, mlp_norm_g, w_up, w_down):
    batch, seq, _ = x.shape
    depth = w_in.shape[0]
    assert seq % BLOCK == 0 and (batch * seq) % TOKEN_TILE == 0
    bias = _bias_table(rel_bias)
    x2 = x.reshape(batch * seq, D_MODEL)
    for l in range(depth):
        w_pad = jnp.pad(w_in[l], ((0, 0), (0, D_IN_PAD - w_in.shape[2]))).astype(BF16)
        q, kv, z, xbc, dt = _in_proj(x2, mix_norm_g[l][None, :], w_pad)
        attn = _attention(q, kv, bias, q_gain[l][None, :], k_gain[l][None, :], sinks[l], batch, seq)
        ssm = _ssd(z, xbc, dt, conv_w[l], conv_b[l][None, :],
                   _pad_lanes(dt_bias[l], DT_PAD), _pad_lanes(a_log[l], DT_PAD),
                   jnp.repeat(d_skip[l], SSM_HEAD_DIM)[None, :], ssm_norm_g[l][None, :], batch, seq)
        x2 = _out_mlp(x2, attn, ssm, w_out[l].astype(BF16), mlp_norm_g[l][None, :],
                      w_up[l].astype(BF16), w_down[l].astype(BF16))
    return x2.reshape(batch, seq, D_MODEL)
```

```python
import math

import numpy as np
import jax
import jax.numpy as jnp
from jax import lax
from jax.experimental import pallas as pl
from jax.experimental.pallas import tpu as pltpu

D_MODEL = 1024
HEAD_DIM = 64
N_Q_HEADS = 8
N_KV_HEADS = 2
Q_PER_KV = N_Q_HEADS // N_KV_HEADS
WINDOW = 128
BLOCK = 128
N_BUCKETS = 32
MAX_DISTANCE = 128
SSM_HEADS = 8
SSM_HEAD_DIM = 64
SSM_GROUPS = 2
HEADS_PER_GROUP = SSM_HEADS // SSM_GROUPS
SSM_STATE = 128
CONV_WIDTH = 4
CHUNK = 128
D_FF = 4 * D_MODEL
D_ATTN = N_Q_HEADS * HEAD_DIM
D_KV = N_KV_HEADS * HEAD_DIM
D_SSM = SSM_HEADS * SSM_HEAD_DIM
D_BC = SSM_GROUPS * SSM_STATE
D_CONV = D_SSM + 2 * D_BC
D_GROUP = D_SSM // SSM_GROUPS
EPS = 1e-6

LANES = 128
SUBLANES = 8
DT_PAD = LANES
COL_Q = 0
COL_KV = D_ATTN
COL_Z = COL_KV + 2 * D_KV
COL_XBC = COL_Z + D_SSM
COL_DT = COL_XBC + D_CONV
D_IN_PAD = COL_DT + DT_PAD
Q_STACK = Q_PER_KV * BLOCK

ATTN_BLOCKS = 4
TOKEN_TILE = 512
FF_TILE = 1024
VMEM_LIMIT = 56 * 1024 * 1024

F32 = jnp.float32
BF16 = jnp.bfloat16


def _rms_scale(x):
    return x * lax.rsqrt(jnp.mean(jnp.square(x), axis=-1, keepdims=True) + EPS)


def _resident(shape):
    nd = len(shape)
    return pl.BlockSpec(shape, lambda *_: (0,) * nd, pipeline_mode=pl.Buffered(1))


def _band_bucket_t():
    kj = np.arange(2 * BLOCK)[:, None]
    qi = np.arange(BLOCK)[None, :]
    dist = qi + BLOCK - kj
    in_window = (dist >= 0) & (dist < WINDOW)
    d = np.clip(dist, 0, None)
    max_exact = N_BUCKETS // 2
    d_f = np.maximum(d, 1).astype(np.float32)
    large = max_exact + (np.log(d_f / np.float32(max_exact)) / np.float32(math.log(MAX_DISTANCE / max_exact))
                         * np.float32(N_BUCKETS - max_exact)).astype(np.int32)
    large = np.minimum(large, N_BUCKETS - 1)
    bucket = np.where(d < max_exact, d, large)
    return np.where(in_window, bucket, -1).astype(np.int32)


def _bias_table_kernel(rel_ref, bucket_ref, out_ref):
    bucket = bucket_ref[...]
    has_prev = lax.broadcasted_iota(jnp.int32, bucket.shape, 0) >= BLOCK
    for h in range(N_Q_HEADS):
        acc = jnp.full(bucket.shape, -jnp.inf, F32)
        for b in range(N_BUCKETS):
            acc = jnp.where(bucket == b, rel_ref[b, h], acc)
        g, gi = divmod(h, Q_PER_KV)
        out_ref[1, g, :, gi * BLOCK:(gi + 1) * BLOCK] = acc
        out_ref[0, g, :, gi * BLOCK:(gi + 1) * BLOCK] = jnp.where(has_prev, acc, -jnp.inf)


def _bias_table(rel_bias):
    bucket = jnp.asarray(_band_bucket_t())
    return pl.pallas_call(
        _bias_table_kernel,
        out_shape=jax.ShapeDtypeStruct((2, N_KV_HEADS, 2 * BLOCK, Q_STACK), F32),
        in_specs=[pl.BlockSpec(memory_space=pltpu.SMEM),
                  pl.BlockSpec(memory_space=pltpu.VMEM)],
        out_specs=pl.BlockSpec(memory_space=pltpu.VMEM),
        name="bias_table",
    )(rel_bias, bucket)


def _head_rms(x):
    lane = lax.broadcasted_iota(jnp.int32, (x.shape[0], LANES), 1)
    low = lane < HEAD_DIM
    parts = []
    for c in range(x.shape[1] // LANES):
        blk = x[:, c * LANES:(c + 1) * LANES]
        sq = blk * blk
        ms_lo = jnp.sum(jnp.where(low, sq, 0.0), axis=-1, keepdims=True) * (1.0 / HEAD_DIM)
        ms_hi = jnp.sum(jnp.where(low, 0.0, sq), axis=-1, keepdims=True) * (1.0 / HEAD_DIM)
        parts.append(blk * lax.rsqrt(jnp.where(low, ms_lo, ms_hi) + EPS))
    return parts


def _in_proj_kernel(x_ref, g_ref, w_ref, qg_ref, kg_ref, q_ref, kv_ref, z_ref, xbc_ref, dt_ref):
    h = (_rms_scale(x_ref[...]) * g_ref[...]).astype(BF16)
    proj = jnp.dot(h, w_ref[...], preferred_element_type=F32)
    q = [p * qg_ref[...] for p in _head_rms(proj[:, COL_Q:COL_KV])]
    q_ref[...] = jnp.concatenate(q, axis=-1).astype(BF16)
    k = [p * kg_ref[...] for p in _head_rms(proj[:, COL_KV:COL_KV + D_KV])]
    kv_ref[...] = jnp.concatenate(k + [proj[:, COL_KV + D_KV:COL_Z]], axis=-1).astype(BF16)
    z_ref[...] = proj[:, COL_Z:COL_XBC].astype(BF16)
    xbc_ref[...] = proj[:, COL_XBC:COL_DT].astype(BF16)
    dt_ref[...] = proj[:, COL_DT:D_IN_PAD]


def _in_proj(x2, g, w_pad, q_gain2, k_gain2):
    t = x2.shape[0]
    row = lambda i: (i, 0)
    widths = (D_ATTN, 2 * D_KV, D_SSM, D_CONV)
    return pl.pallas_call(
        _in_proj_kernel,
        grid=(t // TOKEN_TILE,),
        in_specs=[pl.BlockSpec((TOKEN_TILE, D_MODEL), row),
                  _resident((1, D_MODEL)),
                  _resident((D_MODEL, D_IN_PAD)),
                  _resident((1, LANES)),
                  _resident((1, LANES))],
        out_specs=[pl.BlockSpec((TOKEN_TILE, w), row) for w in widths]
                  + [pl.BlockSpec((TOKEN_TILE, DT_PAD), row)],
        out_shape=[jax.ShapeDtypeStruct((t, w), BF16) for w in widths]
                  + [jax.ShapeDtypeStruct((t, DT_PAD), F32)],
        compiler_params=pltpu.CompilerParams(
            dimension_semantics=("arbitrary",), vmem_limit_bytes=VMEM_LIMIT),
        name="in_proj",
    )(x2, g, w_pad, q_gain2, k_gain2)


def _attention_kernel(q_ref, kvp_ref, kvc_ref, bias_ref, sink_ref, o_ref):
    first_step = pl.program_id(1) == 0
    kv = jnp.concatenate([kvp_ref[...], kvc_ref[...]], axis=0)
    v_t = kv[:, D_KV:].astype(F32).T.astype(BF16)
    for j in range(ATTN_BLOCKS):
        q = q_ref[j * BLOCK:(j + 1) * BLOCK, :]
        band = slice(j * BLOCK, (j + 2) * BLOCK)
        variant = jnp.where(first_step, 0, 1) if j == 0 else 1
        out_t = []
        for g in range(N_KV_HEADS):
            k = kv[band, g * HEAD_DIM:(g + 1) * HEAD_DIM]
            heads = range(g * Q_PER_KV, (g + 1) * Q_PER_KV)
            qs = jnp.concatenate([q[:, h * HEAD_DIM:(h + 1) * HEAD_DIM] for h in heads], axis=0)
            s = lax.dot_general(k, qs, (((1,), (1,)), ((), ())), preferred_element_type=F32)
            s = s + bias_ref[variant, g]
            sink = sink_ref[g]
            m = jnp.maximum(jnp.max(s, axis=0, keepdims=True), sink)
            p = jnp.exp(s - m)
            denom = jnp.sum(p, axis=0, keepdims=True) + jnp.exp(sink - m)
            o = jnp.dot(v_t[g * HEAD_DIM:(g + 1) * HEAD_DIM, band], p.astype(BF16),
                        preferred_element_type=F32)
            out_t.append(o / denom)
        cols = []
        for h0 in range(0, N_Q_HEADS, 2):
            pair = []
            for h in (h0, h0 + 1):
                g, gi = divmod(h, Q_PER_KV)
                pair.append(out_t[g][:, gi * BLOCK:(gi + 1) * BLOCK])
            cols.append(jnp.concatenate(pair, axis=0).T)
        o_ref[j * BLOCK:(j + 1) * BLOCK, :] = jnp.concatenate(cols, axis=-1).astype(BF16)


def _attention(q, kv, bias_t, sink_rows, batch, seq):
    rows = ATTN_BLOCKS * BLOCK
    steps = seq // rows
    cur = lambda b, n: (b * steps + n, 0)
    prev = lambda b, n: ((b * steps + n) * ATTN_BLOCKS - jnp.where(n > 0, 1, 0), 0)
    return pl.pallas_call(
        _attention_kernel,
        grid=(batch, steps),
        in_specs=[pl.BlockSpec((rows, D_ATTN), cur),
                  pl.BlockSpec((BLOCK, 2 * D_KV), prev),
                  pl.BlockSpec((rows, 2 * D_KV), cur),
                  _resident((2, N_KV_HEADS, 2 * BLOCK, Q_STACK)),
                  _resident((N_KV_HEADS, 1, Q_STACK))],
        out_specs=pl.BlockSpec((rows, D_ATTN), cur),
        out_shape=jax.ShapeDtypeStruct((batch * seq, D_ATTN), BF16),
        compiler_params=pltpu.CompilerParams(
            dimension_semantics=("arbitrary", "arbitrary"), vmem_limit_bytes=VMEM_LIMIT),
        name="attention",
    )(q, kv, kv, bias_t, sink_rows)


def _split3(x):
    hi = x.astype(BF16)
    r = x - hi.astype(F32)
    mid = r.astype(BF16)
    lo = (r - mid.astype(F32)).astype(BF16)
    return hi, mid, lo


def _expand_heads(xcol):
    lane = lax.broadcasted_iota(jnp.int32, (CHUNK, LANES), 1)
    parts = []
    for j in range(SSM_HEADS // 2):
        lo = jnp.broadcast_to(xcol[:, 2 * j:2 * j + 1], (CHUNK, LANES))
        hi = jnp.broadcast_to(xcol[:, 2 * j + 1:2 * j + 2], (CHUNK, LANES))
        parts.append(jnp.where(lane < SSM_HEAD_DIM, lo, hi))
    return jnp.concatenate(parts, axis=-1)


def _ssd_kernel(z_ref, xbc_ref, dt_ref, cw_ref, cb_ref, dtb_ref, alog_ref, dsk_ref, ng_ref,
                y_ref, ext_ref, state_ref):
    c = pl.program_id(1)

    @pl.when(c == 0)
    def _():
        ext_ref[0:SUBLANES, :] = jnp.zeros((SUBLANES, D_CONV), F32)
        state_ref[...] = jnp.zeros(state_ref.shape, F32)

    ext_ref[SUBLANES:SUBLANES + CHUNK, :] = xbc_ref[...].astype(F32)
    conv = jnp.broadcast_to(cb_ref[...], (CHUNK, D_CONV))
    for k in range(CONV_WIDTH):
        start = SUBLANES - (CONV_WIDTH - 1) + k
        conv = conv + cw_ref[k:k + 1, :] * ext_ref[start:start + CHUNK, :]
    ext_ref[0:SUBLANES, :] = ext_ref[CHUNK:CHUNK + SUBLANES, :]
    xbc = conv * jax.nn.sigmoid(conv)
    xs = xbc[:, :D_SSM]

    dt = jax.nn.softplus(dt_ref[...] + dtb_ref[...])
    d_a = dt * -jnp.exp(alog_ref[...])
    row = lax.broadcasted_iota(jnp.int32, (CHUNK, CHUNK), 0)
    colm = lax.broadcasted_iota(jnp.int32, (CHUNK, CHUNK), 1)
    causal = row >= colm
    tril = causal.astype(BF16)
    a_cs = sum(jnp.dot(tril, piece, preferred_element_type=F32) for piece in _split3(d_a))
    a_cs_t = a_cs.T

    dt_x = _expand_heads(dt)
    a_cs_x = _expand_heads(a_cs)
    a_last_x = a_cs_x[CHUNK - 1:CHUNK, :]
    xdt = xs * dt_x
    x_end = (xdt * jnp.exp(a_last_x - a_cs_x)).astype(BF16)
    chunk_decay = jnp.exp(a_last_x)
    from_start = jnp.exp(a_cs_x)
    xdt_b = xdt.astype(BF16)

    ys = []
    for g in range(SSM_GROUPS):
        bm = xbc[:, D_SSM + g * SSM_STATE:D_SSM + (g + 1) * SSM_STATE]
        cm = xbc[:, D_SSM + D_BC + g * SSM_STATE:D_SSM + D_BC + (g + 1) * SSM_STATE].astype(BF16)
        bm_b = bm.astype(BF16)
        cb = lax.dot_general(cm, bm_b, (((1,), (1,)), ((), ())), preferred_element_type=F32)
        gs = slice(g * D_GROUP, (g + 1) * D_GROUP)
        prev = state_ref[g]
        y_off = jnp.dot(cm, prev.astype(BF16), preferred_element_type=F32) * from_start[:, gs]
        new = jnp.dot(bm.T.astype(BF16), x_end[:, gs], preferred_element_type=F32)
        state_ref[g] = prev * chunk_decay[:, gs] + new
        y_heads = []
        for r in range(HEADS_PER_GROUP):
            h = g * HEADS_PER_GROUP + r
            seg = a_cs[:, h:h + 1] - a_cs_t[h:h + 1, :]
            decay = jnp.exp(jnp.where(causal, seg, -jnp.inf))
            hs = slice(h * SSM_HEAD_DIM, (h + 1) * SSM_HEAD_DIM)
            y_heads.append(jnp.dot((cb * decay).astype(BF16), xdt_b[:, hs], preferred_element_type=F32))
        ys.append(jnp.concatenate(y_heads, axis=-1) + y_off)
    y = jnp.concatenate(ys, axis=-1) + xs * dsk_ref[...]
    z = z_ref[...].astype(F32)
    y = y * (z * jax.nn.sigmoid(z))
    y = jnp.concatenate([_rms_scale(y[:, g * D_GROUP:(g + 1) * D_GROUP]) for g in range(SSM_GROUPS)], axis=-1)
    y_ref[...] = (y * ng_ref[...]).astype(BF16)


def _ssd(z, xbc, dt, conv_w, conv_b, dt_bias, a_log, d_skip, norm_g, batch, seq):
    nc = seq // CHUNK
    cur = lambda b, c: (b * nc + c, 0)
    return pl.pallas_call(
        _ssd_kernel,
        grid=(batch, nc),
        in_specs=[pl.BlockSpec((CHUNK, D_SSM), cur),
                  pl.BlockSpec((CHUNK, D_CONV), cur),
                  pl.BlockSpec((CHUNK, DT_PAD), cur),
                  _resident((CONV_WIDTH, D_CONV)),
                  _resident((1, D_CONV)),
                  _resident((1, DT_PAD)),
                  _resident((1, DT_PAD)),
                  _resident((1, D_SSM)),
                  _resident((1, D_SSM))],
        out_specs=pl.BlockSpec((CHUNK, D_SSM), cur),
        out_shape=jax.ShapeDtypeStruct((batch * seq, D_SSM), BF16),
        scratch_shapes=[pltpu.VMEM((CHUNK + SUBLANES, D_CONV), F32),
                        pltpu.VMEM((SSM_GROUPS, SSM_STATE, D_GROUP), F32)],
        compiler_params=pltpu.CompilerParams(
            dimension_semantics=("arbitrary", "arbitrary"), vmem_limit_bytes=VMEM_LIMIT),
        name="ssd",
    )(z, xbc, dt, conv_w, conv_b, dt_bias, a_log, d_skip, norm_g)


def _out_mlp_kernel(x_ref, attn_ref, ssm_ref, wo_ref, g_ref, wu_ref, wd_ref, o_ref):
    mix = jnp.dot(attn_ref[...], wo_ref[:D_ATTN, :], preferred_element_type=F32)
    mix = mix + jnp.dot(ssm_ref[...], wo_ref[D_ATTN:, :], preferred_element_type=F32)
    x1 = x_ref[...] + mix
    h = (_rms_scale(x1) * g_ref[...]).astype(BF16)
    acc = x1
    for j in range(D_FF // FF_TILE):
        u = jnp.dot(h, wu_ref[:, j * FF_TILE:(j + 1) * FF_TILE], preferred_element_type=F32)
        u = jnp.square(jnp.maximum(u, 0.0)).astype(BF16)
        acc = acc + jnp.dot(u, wd_ref[j * FF_TILE:(j + 1) * FF_TILE, :], preferred_element_type=F32)
    o_ref[...] = acc


def _out_mlp(x2, attn, ssm, w_out, g, w_up, w_down):
    t = x2.shape[0]
    row = lambda i: (i, 0)
    return pl.pallas_call(
        _out_mlp_kernel,
        grid=(t // TOKEN_TILE,),
        in_specs=[pl.BlockSpec((TOKEN_TILE, D_MODEL), row),
                  pl.BlockSpec((TOKEN_TILE, D_ATTN), row),
                  pl.BlockSpec((TOKEN_TILE, D_SSM), row),
                  _resident((D_ATTN + D_SSM, D_MODEL)),
                  _resident((1, D_MODEL)),
                  _resident((D_MODEL, D_FF)),
                  _resident((D_FF, D_MODEL))],
        out_specs=pl.BlockSpec((TOKEN_TILE, D_MODEL), row),
        out_shape=jax.ShapeDtypeStruct((t, D_MODEL), F32),
        compiler_params=pltpu.CompilerParams(
            dimension_semantics=("arbitrary",), vmem_limit_bytes=VMEM_LIMIT),
        name="out_mlp",
    )(x2, attn, ssm, w_out, g, w_up, w_down)


def _pad_lanes(v, width):
    return jnp.pad(v, (0, width - v.shape[0]))[None, :]


def kernel(x, mix_norm_g, w_in, q_gain, k_gain, sinks, rel_bias, conv_w, conv_b, dt_bias, a_log,
           d_skip, ssm_norm_g, w_out, mlp_norm_g, w_up, w_down):
    batch, seq, _ = x.shape
    depth = w_in.shape[0]
    assert seq % BLOCK == 0 and (batch * seq) % TOKEN_TILE == 0
    bias_t = _bias_table(rel_bias)
    x2 = x.reshape(batch * seq, D_MODEL)
    for l in range(depth):
        w_pad = jnp.pad(w_in[l], ((0, 0), (0, D_IN_PAD - w_in.shape[2]))).astype(BF16)
        q_gain2 = jnp.tile(q_gain[l] * (HEAD_DIM ** -0.5), 2)[None, :]
        k_gain2 = jnp.tile(k_gain[l], 2)[None, :]
        q, kv, z, xbc, dt = _in_proj(x2, mix_norm_g[l][None, :], w_pad, q_gain2, k_gain2)
        sink_rows = jnp.repeat(sinks[l].reshape(N_KV_HEADS, 1, Q_PER_KV), BLOCK, axis=-1)
        attn = _attention(q, kv, bias_t, sink_rows, batch, seq)
        ssm = _ssd(z, xbc, dt, conv_w[l], conv_b[l][None, :],
                   _pad_lanes(dt_bias[l], DT_PAD), _pad_lanes(a_log[l], DT_PAD),
                   jnp.repeat(d_skip[l], SSM_HEAD_DIM)[None, :], ssm_norm_g[l][None, :], batch, seq)
        x2 = _out_mlp(x2, attn, ssm, w_out[l].astype(BF16), mlp_norm_g[l][None, :],
                      w_up[l].astype(BF16), w_down[l].astype(BF16))
    return x2.reshape(batch, seq, D_MODEL)
```

```python
import functools
import math

import numpy as np
import jax
import jax.numpy as jnp
from jax import lax
from jax.experimental import pallas as pl
from jax.experimental.pallas import tpu as pltpu

D_MODEL = 1024
HEAD_DIM = 64
N_Q_HEADS = 8
N_KV_HEADS = 2
Q_PER_KV = N_Q_HEADS // N_KV_HEADS
WINDOW = 128
BLOCK = 128
N_BUCKETS = 32
MAX_DISTANCE = 128
SSM_HEADS = 8
SSM_HEAD_DIM = 64
SSM_GROUPS = 2
HEADS_PER_GROUP = SSM_HEADS // SSM_GROUPS
SSM_STATE = 128
CONV_WIDTH = 4
CHUNK = 128
D_FF = 4 * D_MODEL
D_ATTN = N_Q_HEADS * HEAD_DIM
D_KV = N_KV_HEADS * HEAD_DIM
D_SSM = SSM_HEADS * SSM_HEAD_DIM
D_BC = SSM_GROUPS * SSM_STATE
D_CONV = D_SSM + 2 * D_BC
D_GROUP = D_SSM // SSM_GROUPS
EPS = 1e-6

LANES = 128
SUBLANES = 8
DT_PAD = LANES
COL_XBC = 0
COL_DT = COL_XBC + D_CONV
COL_Q = COL_DT + DT_PAD
COL_KV = COL_Q + D_ATTN
COL_Z = COL_KV + 2 * D_KV
D_IN_PAD = COL_Z + D_SSM
SRC_KV = D_ATTN
SRC_Z = SRC_KV + 2 * D_KV
SRC_XBC = SRC_Z + D_SSM
SRC_DT = SRC_XBC + D_CONV
Q_STACK = Q_PER_KV * BLOCK

ATTN_BLOCKS = 4
SSD_CHUNKS = 4
TOKEN_TILE = 512
FF_TILE = 1024
VMEM_LIMIT = 56 * 1024 * 1024

F32 = jnp.float32
BF16 = jnp.bfloat16


def _rms_scale(x):
    return x * lax.rsqrt(jnp.mean(jnp.square(x), axis=-1, keepdims=True) + EPS)


def _resident(shape):
    nd = len(shape)
    return pl.BlockSpec(shape, lambda *_: (0,) * nd, pipeline_mode=pl.Buffered(1))


def _band_bucket_t():
    kj = np.arange(2 * BLOCK)[:, None]
    qi = np.arange(BLOCK)[None, :]
    dist = qi + BLOCK - kj
    in_window = (dist >= 0) & (dist < WINDOW)
    d = np.clip(dist, 0, None)
    max_exact = N_BUCKETS // 2
    d_f = np.maximum(d, 1).astype(np.float32)
    large = max_exact + (np.log(d_f / np.float32(max_exact)) / np.float32(math.log(MAX_DISTANCE / max_exact))
                         * np.float32(N_BUCKETS - max_exact)).astype(np.int32)
    large = np.minimum(large, N_BUCKETS - 1)
    bucket = np.where(d < max_exact, d, large)
    return np.where(in_window, bucket, -1).astype(np.int32)


def _bias_table_kernel(rel_ref, bucket_ref, out_ref):
    bucket = bucket_ref[...]
    has_prev = lax.broadcasted_iota(jnp.int32, bucket.shape, 0) >= BLOCK
    for h in range(N_Q_HEADS):
        acc = jnp.full(bucket.shape, -jnp.inf, F32)
        for b in range(N_BUCKETS):
            acc = jnp.where(bucket == b, rel_ref[b, h], acc)
        g, gi = divmod(h, Q_PER_KV)
        out_ref[1, g, :, gi * BLOCK:(gi + 1) * BLOCK] = acc
        out_ref[0, g, :, gi * BLOCK:(gi + 1) * BLOCK] = jnp.where(has_prev, acc, -jnp.inf)


def _bias_table(rel_bias):
    bucket = jnp.asarray(_band_bucket_t())
    return pl.pallas_call(
        _bias_table_kernel,
        out_shape=jax.ShapeDtypeStruct((2, N_KV_HEADS, 2 * BLOCK, Q_STACK), F32),
        in_specs=[pl.BlockSpec(memory_space=pltpu.SMEM),
                  pl.BlockSpec(memory_space=pltpu.VMEM)],
        out_specs=pl.BlockSpec(memory_space=pltpu.VMEM),
        name="bias_table",
    )(rel_bias, bucket)


def _head_rms(x):
    lane = lax.broadcasted_iota(jnp.int32, (x.shape[0], LANES), 1)
    low = lane < HEAD_DIM
    parts = []
    for c in range(x.shape[1] // LANES):
        blk = x[:, c * LANES:(c + 1) * LANES]
        sq = blk * blk
        ms_lo = jnp.sum(jnp.where(low, sq, 0.0), axis=-1, keepdims=True) * (1.0 / HEAD_DIM)
        ms_hi = jnp.sum(jnp.where(low, 0.0, sq), axis=-1, keepdims=True) * (1.0 / HEAD_DIM)
        parts.append(blk * lax.rsqrt(jnp.where(low, ms_lo, ms_hi) + EPS))
    return parts


def _causal_conv_silu(raw, halo, cw_ref, cb_ref):
    sub = lax.broadcasted_iota(jnp.int32, (SUBLANES, D_CONV), 0)
    acc = cb_ref[...] + cw_ref[CONV_WIDTH - 1:CONV_WIDTH, :] * raw
    for j in range(1, CONV_WIDTH):
        rolled = pltpu.roll(raw, j, axis=0)
        first = jnp.where(sub < j, pltpu.roll(halo, j, axis=0), rolled[:SUBLANES])
        shifted = jnp.concatenate([first, rolled[SUBLANES:]], axis=0)
        acc = acc + cw_ref[CONV_WIDTH - 1 - j:CONV_WIDTH - j, :] * shifted
    return acc * jax.nn.sigmoid(acc)


def _in_proj_kernel(tiles_per_seq, x_ref, g_ref, w_ref, qg_ref, kg_ref, cw_ref, cb_ref,
                    q_ref, kv_ref, z_ref, xbc_ref, dt_ref, halo_ref):
    @pl.when(pl.program_id(0) % tiles_per_seq == 0)
    def _():
        halo_ref[...] = jnp.zeros(halo_ref.shape, F32)

    h = (_rms_scale(x_ref[...]) * g_ref[...]).astype(BF16)
    proj = jnp.dot(h, w_ref[...], preferred_element_type=F32)
    raw = proj[:, COL_XBC:COL_DT]
    xbc_ref[...] = _causal_conv_silu(raw, halo_ref[...], cw_ref, cb_ref).astype(BF16)
    halo_ref[...] = raw[TOKEN_TILE - SUBLANES:, :]
    dt_ref[...] = proj[:, COL_DT:COL_Q].T[:SSM_HEADS, :]
    q = [p * qg_ref[...] for p in _head_rms(proj[:, COL_Q:COL_KV])]
    q_ref[...] = jnp.concatenate(q, axis=-1).astype(BF16)
    k = [p * kg_ref[...] for p in _head_rms(proj[:, COL_KV:COL_KV + D_KV])]
    kv_ref[...] = jnp.concatenate(k + [proj[:, COL_KV + D_KV:COL_Z]], axis=-1).astype(BF16)
    z_ref[...] = proj[:, COL_Z:D_IN_PAD].astype(BF16)


def _in_proj(x2, g, w_pad, q_gain2, k_gain2, conv_w, conv_b, seq):
    t = x2.shape[0]
    row = lambda i: (i, 0)
    widths = (D_ATTN, 2 * D_KV, D_SSM, D_CONV)
    return pl.pallas_call(
        functools.partial(_in_proj_kernel, seq // TOKEN_TILE),
        grid=(t // TOKEN_TILE,),
        in_specs=[pl.BlockSpec((TOKEN_TILE, D_MODEL), row),
                  _resident((1, D_MODEL)),
                  _resident((D_MODEL, D_IN_PAD)),
                  _resident((1, LANES)),
                  _resident((1, LANES)),
                  _resident((CONV_WIDTH, D_CONV)),
                  _resident((1, D_CONV))],
        out_specs=[pl.BlockSpec((TOKEN_TILE, w), row) for w in widths]
                  + [pl.BlockSpec((SSM_HEADS, TOKEN_TILE), lambda i: (0, i))],
        out_shape=[jax.ShapeDtypeStruct((t, w), BF16) for w in widths]
                  + [jax.ShapeDtypeStruct((SSM_HEADS, t), F32)],
        scratch_shapes=[pltpu.VMEM((SUBLANES, D_CONV), F32)],
        compiler_params=pltpu.CompilerParams(
            dimension_semantics=("arbitrary",), vmem_limit_bytes=VMEM_LIMIT),
        name="in_proj",
    )(x2, g, w_pad, q_gain2, k_gain2, conv_w, conv_b)


def _attention_kernel(q_ref, kvp_ref, kvc_ref, bias_ref, sink_ref, o_ref):
    first_step = pl.program_id(1) == 0
    kv = jnp.concatenate([kvp_ref[...], kvc_ref[...]], axis=0)
    v_t = kv[:, D_KV:].astype(F32).T.astype(BF16)
    for j in range(ATTN_BLOCKS):
        q = q_ref[j * BLOCK:(j + 1) * BLOCK, :]
        band = slice(j * BLOCK, (j + 2) * BLOCK)
        variant = jnp.where(first_step, 0, 1) if j == 0 else 1
        out_t = []
        for g in range(N_KV_HEADS):
            k = kv[band, g * HEAD_DIM:(g + 1) * HEAD_DIM]
            heads = range(g * Q_PER_KV, (g + 1) * Q_PER_KV)
            qs = jnp.concatenate([q[:, h * HEAD_DIM:(h + 1) * HEAD_DIM] for h in heads], axis=0)
            s = lax.dot_general(k, qs, (((1,), (1,)), ((), ())), preferred_element_type=F32)
            s = s + bias_ref[variant, g]
            sink = sink_ref[g]
            m = jnp.maximum(jnp.max(s, axis=0, keepdims=True), sink)
            p = jnp.exp(s - m)
            denom = jnp.sum(p, axis=0, keepdims=True) + jnp.exp(sink - m)
            o = jnp.dot(v_t[g * HEAD_DIM:(g + 1) * HEAD_DIM, band], p.astype(BF16),
                        preferred_element_type=F32)
            out_t.append(o / denom)
        cols = []
        for h0 in range(0, N_Q_HEADS, 2):
            pair = []
            for h in (h0, h0 + 1):
                g, gi = divmod(h, Q_PER_KV)
                pair.append(out_t[g][:, gi * BLOCK:(gi + 1) * BLOCK])
            cols.append(jnp.concatenate(pair, axis=0).T)
        o_ref[j * BLOCK:(j + 1) * BLOCK, :] = jnp.concatenate(cols, axis=-1).astype(BF16)


def _attention(q, kv, bias_t, sink_rows, batch, seq):
    rows = ATTN_BLOCKS * BLOCK
    steps = seq // rows
    cur = lambda b, n: (b * steps + n, 0)
    prev = lambda b, n: ((b * steps + n) * ATTN_BLOCKS - jnp.where(n > 0, 1, 0), 0)
    return pl.pallas_call(
        _attention_kernel,
        grid=(batch, steps),
        in_specs=[pl.BlockSpec((rows, D_ATTN), cur),
                  pl.BlockSpec((BLOCK, 2 * D_KV), prev),
                  pl.BlockSpec((rows, 2 * D_KV), cur),
                  _resident((2, N_KV_HEADS, 2 * BLOCK, Q_STACK)),
                  _resident((N_KV_HEADS, 1, Q_STACK))],
        out_specs=pl.BlockSpec((rows, D_ATTN), cur),
        out_shape=jax.ShapeDtypeStruct((batch * seq, D_ATTN), BF16),
        compiler_params=pltpu.CompilerParams(
            dimension_semantics=("arbitrary", "arbitrary"), vmem_limit_bytes=VMEM_LIMIT),
        name="attention",
    )(q, kv, kv, bias_t, sink_rows)


def _split3(x):
    hi = x.astype(BF16)
    r = x - hi.astype(F32)
    mid = r.astype(BF16)
    lo = (r - mid.astype(F32)).astype(BF16)
    return hi, mid, lo


def _ssd_kernel(z_ref, xbc_ref, dtr_ref, dtb_ref, alog_ref, dsk_ref, ng_ref, y_ref, state_ref):
    @pl.when(pl.program_id(1) == 0)
    def _():
        state_ref[...] = jnp.zeros(state_ref.shape, F32)

    low = lax.broadcasted_iota(jnp.int32, (CHUNK, LANES), 1) < SSM_HEAD_DIM
    row = lax.broadcasted_iota(jnp.int32, (CHUNK, CHUNK), 0)
    col = lax.broadcasted_iota(jnp.int32, (CHUNK, CHUNK), 1)
    causal = row >= col
    triu = (row <= col).astype(BF16)
    neg_a = jnp.exp(alog_ref[...])

    for c in range(SSD_CHUNKS):
        rs = slice(c * CHUNK, (c + 1) * CHUNK)
        dt = jax.nn.softplus(dtr_ref[:, rs] + dtb_ref[...])
        d_a = -(dt * neg_a)
        a_cs = sum(jnp.dot(piece, triu, preferred_element_type=F32) for piece in _split3(d_a))
        a_last = jnp.broadcast_to(a_cs[:, CHUNK - 1:CHUNK], a_cs.shape)
        w_end = dt * jnp.exp(a_last - a_cs)
        chunk_decay = jnp.exp(a_last)

        xs = xbc_ref[rs, :D_SSM]
        y_groups = []
        for g in range(SSM_GROUPS):
            bm = xbc_ref[rs, D_SSM + g * SSM_STATE:D_SSM + (g + 1) * SSM_STATE]
            cm = xbc_ref[rs, D_SSM + D_BC + g * SSM_STATE:D_SSM + D_BC + (g + 1) * SSM_STATE]
            bm_t = bm.astype(F32).T
            cb = jnp.dot(cm, bm_t.astype(BF16), preferred_element_type=F32)
            prev = state_ref[g]
            y_off = jnp.dot(cm, prev.astype(BF16), preferred_element_type=F32)
            y_tiles, new_tiles = [], []
            for pr in range(HEADS_PER_GROUP // 2):
                h0 = g * HEADS_PER_GROUP + 2 * pr
                tile = slice((h0 // 2) * LANES, (h0 // 2 + 1) * LANES)
                xs_t = xs[:, tile]
                zero = jnp.zeros_like(xs_t)
                rhs = jnp.concatenate([jnp.where(low, xs_t, zero), jnp.where(low, zero, xs_t)], axis=0)
                m_parts, b_parts, e_parts = [], [], []
                for h in (h0, h0 + 1):
                    a_row = a_cs[h:h + 1, :]
                    a_col = jnp.broadcast_to(a_row, (CHUNK, CHUNK)).T
                    decay = jnp.exp(jnp.where(causal, a_col - a_row, -jnp.inf))
                    m_parts.append((cb * decay * dt[h:h + 1, :]).astype(BF16))
                    b_parts.append((bm_t * w_end[h:h + 1, :]).astype(BF16))
                    e_parts.append(jnp.exp(a_col))
                y_diag = jnp.dot(jnp.concatenate(m_parts, axis=1), rhs, preferred_element_type=F32)
                new = jnp.dot(jnp.concatenate(b_parts, axis=1), rhs, preferred_element_type=F32)
                ps = slice(pr * LANES, (pr + 1) * LANES)
                y_tiles.append(y_diag + y_off[:, ps] * jnp.where(low, e_parts[0], e_parts[1]))
                cd = jnp.where(low[:1], chunk_decay[h0:h0 + 1, :], chunk_decay[h0 + 1:h0 + 2, :])
                new_tiles.append(prev[:, ps] * cd + new)
            state_ref[g] = jnp.concatenate(new_tiles, axis=1)
            y_groups.append(jnp.concatenate(y_tiles, axis=1))
        y = jnp.concatenate(y_groups, axis=1) + xs.astype(F32) * dsk_ref[...]
        z = z_ref[rs, :].astype(F32)
        y = y * (z * jax.nn.sigmoid(z))
        y = jnp.concatenate([_rms_scale(y[:, g * D_GROUP:(g + 1) * D_GROUP]) for g in range(SSM_GROUPS)],
                            axis=-1)
        y_ref[rs, :] = (y * ng_ref[...]).astype(BF16)


def _ssd(z, xbc, dt_rows, dt_bias, a_log, d_skip, norm_g, batch, seq):
    rows = SSD_CHUNKS * CHUNK
    steps = seq // rows
    cur = lambda b, c: (b * steps + c, 0)
    return pl.pallas_call(
        _ssd_kernel,
        grid=(batch, steps),
        in_specs=[pl.BlockSpec((rows, D_SSM), cur),
                  pl.BlockSpec((rows, D_CONV), cur),
                  pl.BlockSpec((SSM_HEADS, rows), lambda b, c: (0, b * steps + c)),
                  _resident((SSM_HEADS, CHUNK)),
                  _resident((SSM_HEADS, CHUNK)),
                  _resident((1, D_SSM)),
                  _resident((1, D_SSM))],
        out_specs=pl.BlockSpec((rows, D_SSM), cur),
        out_shape=jax.ShapeDtypeStruct((batch * seq, D_SSM), BF16),
        scratch_shapes=[pltpu.VMEM((SSM_GROUPS, SSM_STATE, D_GROUP), F32)],
        compiler_params=pltpu.CompilerParams(
            dimension_semantics=("arbitrary", "arbitrary"), vmem_limit_bytes=VMEM_LIMIT),
        name="ssd",
    )(z, xbc, dt_rows, dt_bias, a_log, d_skip, norm_g)


def _out_mlp_kernel(x_ref, attn_ref, ssm_ref, wo_ref, g_ref, wu_ref, wd_ref, o_ref):
    mix = jnp.dot(attn_ref[...], wo_ref[:D_ATTN, :], preferred_element_type=F32)
    mix = mix + jnp.dot(ssm_ref[...], wo_ref[D_ATTN:, :], preferred_element_type=F32)
    x1 = x_ref[...] + mix
    h = (_rms_scale(x1) * g_ref[...]).astype(BF16)
    acc = x1
    for j in range(D_FF // FF_TILE):
        u = jnp.dot(h, wu_ref[:, j * FF_TILE:(j + 1) * FF_TILE], preferred_element_type=F32)
        u = jnp.square(jnp.maximum(u, 0.0)).astype(BF16)
        acc = acc + jnp.dot(u, wd_ref[j * FF_TILE:(j + 1) * FF_TILE, :], preferred_element_type=F32)
    o_ref[...] = acc


def _out_mlp(x2, attn, ssm, w_out, g, w_up, w_down):
    t = x2.shape[0]
    row = lambda i: (i, 0)
    return pl.pallas_call(
        _out_mlp_kernel,
        grid=(t // TOKEN_TILE,),
        in_specs=[pl.BlockSpec((TOKEN_TILE, D_MODEL), row),
                  pl.BlockSpec((TOKEN_TILE, D_ATTN), row),
                  pl.BlockSpec((TOKEN_TILE, D_SSM), row),
                  _resident((D_ATTN + D_SSM, D_MODEL)),
                  _resident((1, D_MODEL)),
                  _resident((D_MODEL, D_FF)),
                  _resident((D_FF, D_MODEL))],
        out_specs=pl.BlockSpec((TOKEN_TILE, D_MODEL), row),
        out_shape=jax.ShapeDtypeStruct((t, D_MODEL), F32),
        compiler_params=pltpu.CompilerParams(
            dimension_semantics=("arbitrary",), vmem_limit_bytes=VMEM_LIMIT),
        name="out_mlp",
    )(x2, attn, ssm, w_out, g, w_up, w_down)


def _head_rows(v):
    return jnp.broadcast_to(v[:, None], (SSM_HEADS, CHUNK))


def kernel(x, mix_norm_g, w_in, q_gain, k_gain, sinks, rel_bias, conv_w, conv_b, dt_bias, a_log,
           d_skip, ssm_norm_g, w_out, mlp_norm_g, w_up, w_down):
    batch, seq, _ = x.shape
    depth = w_in.shape[0]
    assert seq % TOKEN_TILE == 0 and seq % (ATTN_BLOCKS * BLOCK) == 0 and seq % (SSD_CHUNKS * CHUNK) == 0
    bias_t = _bias_table(rel_bias)
    x2 = x.reshape(batch * seq, D_MODEL)
    for l in range(depth):
        w = w_in[l]
        w_pad = jnp.concatenate(
            [w[:, SRC_XBC:SRC_DT], jnp.pad(w[:, SRC_DT:], ((0, 0), (0, DT_PAD - SSM_HEADS))),
             w[:, :SRC_XBC]], axis=1).astype(BF16)
        q_gain2 = jnp.tile(q_gain[l] * (HEAD_DIM ** -0.5), 2)[None, :]
        k_gain2 = jnp.tile(k_gain[l], 2)[None, :]
        q, kv, z, xbc, dt_rows = _in_proj(x2, mix_norm_g[l][None, :], w_pad, q_gain2, k_gain2,
                                          conv_w[l], conv_b[l][None, :], seq)
        sink_rows = jnp.repeat(sinks[l].reshape(N_KV_HEADS, 1, Q_PER_KV), BLOCK, axis=-1)
        attn = _attention(q, kv, bias_t, sink_rows, batch, seq)
        ssm = _ssd(z, xbc, dt_rows, _head_rows(dt_bias[l]), _head_rows(a_log[l]),
                   jnp.repeat(d_skip[l], SSM_HEAD_DIM)[None, :], ssm_norm_g[l][None, :], batch, seq)
        x2 = _out_mlp(x2, attn, ssm, w_out[l].astype(BF16), mlp_norm_g[l][None, :],
                      w_up[l].astype(BF16), w_down[l].astype(BF16))
    return x2.reshape(batch, seq, D_MODEL)
```

```python
import functools
import math

import numpy as np
import jax
import jax.numpy as jnp
from jax import lax
from jax.experimental import pallas as pl
from jax.experimental.pallas import tpu as pltpu

D_MODEL = 1024
HEAD_DIM = 64
N_Q_HEADS = 8
N_KV_HEADS = 2
Q_PER_KV = N_Q_HEADS // N_KV_HEADS
WINDOW = 128
BLOCK = 128
N_BUCKETS = 32
MAX_DISTANCE = 128
SSM_HEADS = 8
SSM_HEAD_DIM = 64
SSM_GROUPS = 2
HEADS_PER_GROUP = SSM_HEADS // SSM_GROUPS
SSM_STATE = 128
CONV_WIDTH = 4
CHUNK = 128
D_FF = 4 * D_MODEL
D_ATTN = N_Q_HEADS * HEAD_DIM
D_KV = N_KV_HEADS * HEAD_DIM
D_SSM = SSM_HEADS * SSM_HEAD_DIM
D_BC = SSM_GROUPS * SSM_STATE
D_CONV = D_SSM + 2 * D_BC
D_GROUP = D_SSM // SSM_GROUPS
D_MIX = D_ATTN + D_SSM
EPS = 1e-6
assert BLOCK == CHUNK

LANES = 128
SUBLANES = 8
DT_PAD = LANES
COL_Q = 0
COL_KV = D_ATTN
COL_Z = COL_KV + 2 * D_KV
COL_XBC = COL_Z + D_SSM
COL_DT = COL_XBC + D_CONV
D_IN_PAD = COL_DT + DT_PAD
Q_STACK = Q_PER_KV * BLOCK

TOKEN_TILE = 512
SUB_TILES = TOKEN_TILE // BLOCK
FF_TILE = 512
DOWN_TILE = D_MODEL // SUB_TILES
VMEM_LIMIT = 56 * 1024 * 1024

F32 = jnp.float32
BF16 = jnp.bfloat16


def _rms_scale(x):
    return x * lax.rsqrt(jnp.mean(jnp.square(x), axis=-1, keepdims=True) + EPS)


def _resident(shape):
    nd = len(shape)
    return pl.BlockSpec(shape, lambda *_: (0,) * nd, pipeline_mode=pl.Buffered(1))


def _band_bucket_t():
    kj = np.arange(2 * BLOCK)[:, None]
    qi = np.arange(BLOCK)[None, :]
    dist = qi + BLOCK - kj
    in_window = (dist >= 0) & (dist < WINDOW)
    d = np.clip(dist, 0, None)
    max_exact = N_BUCKETS // 2
    d_f = np.maximum(d, 1).astype(np.float32)
    large = max_exact + (np.log(d_f / np.float32(max_exact)) / np.float32(math.log(MAX_DISTANCE / max_exact))
                         * np.float32(N_BUCKETS - max_exact)).astype(np.int32)
    large = np.minimum(large, N_BUCKETS - 1)
    bucket = np.where(d < max_exact, d, large)
    return np.where(in_window, bucket, -1).astype(np.int32)


def _bias_table_kernel(rel_ref, bucket_ref, out_ref):
    bucket = bucket_ref[...]
    has_prev = lax.broadcasted_iota(jnp.int32, bucket.shape, 0) >= BLOCK
    for h in range(N_Q_HEADS):
        acc = jnp.full(bucket.shape, -jnp.inf, F32)
        for b in range(N_BUCKETS):
            acc = jnp.where(bucket == b, rel_ref[b, h], acc)
        g, gi = divmod(h, Q_PER_KV)
        out_ref[1, g, :, gi * BLOCK:(gi + 1) * BLOCK] = acc
        out_ref[0, g, :, gi * BLOCK:(gi + 1) * BLOCK] = jnp.where(has_prev, acc, -jnp.inf)


def _bias_table(rel_bias):
    bucket = jnp.asarray(_band_bucket_t())
    return pl.pallas_call(
        _bias_table_kernel,
        out_shape=jax.ShapeDtypeStruct((2, N_KV_HEADS, 2 * BLOCK, Q_STACK), F32),
        in_specs=[pl.BlockSpec(memory_space=pltpu.SMEM),
                  pl.BlockSpec(memory_space=pltpu.VMEM)],
        out_specs=pl.BlockSpec(memory_space=pltpu.VMEM),
        name="bias_table",
    )(rel_bias, bucket)


def _head_rms(x):
    lane = lax.broadcasted_iota(jnp.int32, (x.shape[0], LANES), 1)
    low = lane < HEAD_DIM
    parts = []
    for c in range(x.shape[1] // LANES):
        blk = x[:, c * LANES:(c + 1) * LANES]
        sq = blk * blk
        ms_lo = jnp.sum(jnp.where(low, sq, 0.0), axis=-1, keepdims=True) * (1.0 / HEAD_DIM)
        ms_hi = jnp.sum(jnp.where(low, 0.0, sq), axis=-1, keepdims=True) * (1.0 / HEAD_DIM)
        parts.append(blk * lax.rsqrt(jnp.where(low, ms_lo, ms_hi) + EPS))
    return parts


def _in_proj_kernel(x_ref, g_ref, w_ref, qg_ref, kg_ref, q_ref, kv_ref, z_ref, xbc_ref, dt_ref):
    h = (_rms_scale(x_ref[...]) * g_ref[...]).astype(BF16)
    proj = jnp.dot(h, w_ref[...], preferred_element_type=F32)
    q = [p * qg_ref[...] for p in _head_rms(proj[:, COL_Q:COL_KV])]
    q_ref[...] = jnp.concatenate(q, axis=-1).astype(BF16)
    k = [p * kg_ref[...] for p in _head_rms(proj[:, COL_KV:COL_KV + D_KV])]
    kv_ref[...] = jnp.concatenate(k + [proj[:, COL_KV + D_KV:COL_Z]], axis=-1).astype(BF16)
    z_ref[...] = proj[:, COL_Z:COL_XBC].astype(BF16)
    xbc_ref[...] = proj[:, COL_XBC:COL_DT].astype(BF16)
    dt_ref[...] = proj[:, COL_DT:D_IN_PAD].T[:SSM_HEADS, :]


def _in_proj(x2, g, w_pad, q_gain2, k_gain2):
    t = x2.shape[0]
    row = lambda i: (i, 0)
    widths = (D_ATTN, 2 * D_KV, D_SSM, D_CONV)
    return pl.pallas_call(
        _in_proj_kernel,
        grid=(t // TOKEN_TILE,),
        in_specs=[pl.BlockSpec((TOKEN_TILE, D_MODEL), row),
                  _resident((1, D_MODEL)),
                  _resident((D_MODEL, D_IN_PAD)),
                  _resident((1, LANES)),
                  _resident((1, LANES))],
        out_specs=[pl.BlockSpec((TOKEN_TILE, w), row) for w in widths]
                  + [pl.BlockSpec((SSM_HEADS, TOKEN_TILE), lambda i: (0, i))],
        out_shape=[jax.ShapeDtypeStruct((t, w), BF16) for w in widths]
                  + [jax.ShapeDtypeStruct((SSM_HEADS, t), F32)],
        compiler_params=pltpu.CompilerParams(
            dimension_semantics=("arbitrary",), vmem_limit_bytes=VMEM_LIMIT),
        name="in_proj",
    )(x2, g, w_pad, q_gain2, k_gain2)


def _attention_scores(q, k_band, bias_ref, variant, sink_ref):
    out = []
    for g in range(N_KV_HEADS):
        k = k_band[:, g * HEAD_DIM:(g + 1) * HEAD_DIM]
        heads = range(g * Q_PER_KV, (g + 1) * Q_PER_KV)
        qs = jnp.concatenate([q[:, h * HEAD_DIM:(h + 1) * HEAD_DIM] for h in heads], axis=0)
        s = lax.dot_general(k, qs, (((1,), (1,)), ((), ())), preferred_element_type=F32)
        s = s + bias_ref[variant, g]
        sink = sink_ref[g]
        m = jnp.maximum(jnp.max(s, axis=0, keepdims=True), sink)
        p = jnp.exp(s - m)
        denom = jnp.sum(p, axis=0, keepdims=True) + jnp.exp(sink - m)
        out.append((p.astype(BF16), denom))
    return out


def _attention_values(probs, v_t_band):
    out_t = []
    for g, (p, denom) in enumerate(probs):
        o = jnp.dot(v_t_band[g * HEAD_DIM:(g + 1) * HEAD_DIM, :], p, preferred_element_type=F32)
        out_t.append(o / denom)
    cols = []
    for h0 in range(0, N_Q_HEADS, 2):
        pair = []
        for h in (h0, h0 + 1):
            g, gi = divmod(h, Q_PER_KV)
            pair.append(out_t[g][:, gi * BLOCK:(gi + 1) * BLOCK])
        cols.append(jnp.concatenate(pair, axis=0).T)
    return jnp.concatenate(cols, axis=-1).astype(BF16)


def _causal_conv_silu(raw, halo, cw_ref, cb_ref):
    sub = lax.broadcasted_iota(jnp.int32, (SUBLANES, D_CONV), 0)
    acc = cb_ref[...] + cw_ref[CONV_WIDTH - 1:CONV_WIDTH, :] * raw
    for j in range(1, CONV_WIDTH):
        rolled = pltpu.roll(raw, j, axis=0)
        first = jnp.where(sub < j, pltpu.roll(halo, j, axis=0), rolled[:SUBLANES])
        shifted = jnp.concatenate([first, rolled[SUBLANES:]], axis=0)
        acc = acc + cw_ref[CONV_WIDTH - 1 - j:CONV_WIDTH - j, :] * shifted
    return acc * jax.nn.sigmoid(acc)


def _lane_cumsum(x):
    lane = lax.broadcasted_iota(jnp.int32, x.shape, 1)
    shift = 1
    while shift < x.shape[1]:
        x = x + jnp.where(lane >= shift, pltpu.roll(x, shift, axis=1), 0.0)
        shift *= 2
    return x


def _ssd_prepare(xbc, dt_raw, dtb_ref, alog_ref):
    low = lax.broadcasted_iota(jnp.int32, (CHUNK, LANES), 1) < SSM_HEAD_DIM
    row = lax.broadcasted_iota(jnp.int32, (CHUNK, CHUNK), 0)
    col = lax.broadcasted_iota(jnp.int32, (CHUNK, CHUNK), 1)
    causal = row >= col

    dt = jax.nn.softplus(dt_raw + dtb_ref[...])
    a_cs = _lane_cumsum(-(dt * jnp.exp(alog_ref[...])))
    a_last = jnp.broadcast_to(a_cs[:, CHUNK - 1:CHUNK], a_cs.shape)
    w_end = dt * jnp.exp(a_last - a_cs)
    chunk_decay = jnp.exp(a_last)

    xs = xbc[:, :D_SSM]
    xs_b = xs.astype(BF16)
    prep = {"xs": xs, "cm": [], "bm_t": [], "rhs": [], "decay_dt": [], "b_end": [], "from_start": [],
            "state_decay": []}
    for g in range(SSM_GROUPS):
        bm = xbc[:, D_SSM + g * SSM_STATE:D_SSM + (g + 1) * SSM_STATE]
        bm_t = bm.T
        prep["bm_t"].append(bm_t.astype(BF16))
        prep["cm"].append(xbc[:, D_SSM + D_BC + g * SSM_STATE:D_SSM + D_BC + (g + 1) * SSM_STATE].astype(BF16))
        for pr in range(HEADS_PER_GROUP // 2):
            h0 = g * HEADS_PER_GROUP + 2 * pr
            xs_t = xs_b[:, (h0 // 2) * LANES:(h0 // 2 + 1) * LANES]
            zero = jnp.zeros_like(xs_t)
            prep["rhs"].append(jnp.concatenate([jnp.where(low, xs_t, zero), jnp.where(low, zero, xs_t)], axis=0))
            decays, b_parts, e_parts = [], [], []
            for h in (h0, h0 + 1):
                a_row = a_cs[h:h + 1, :]
                a_col = jnp.broadcast_to(a_row, (CHUNK, CHUNK)).T
                decays.append(jnp.exp(jnp.where(causal, a_col - a_row, -jnp.inf)) * dt[h:h + 1, :])
                b_parts.append((bm_t * w_end[h:h + 1, :]).astype(BF16))
                e_parts.append(jnp.exp(a_col))
            prep["decay_dt"].append(decays)
            prep["b_end"].append(jnp.concatenate(b_parts, axis=1))
            prep["from_start"].append(jnp.where(low, e_parts[0], e_parts[1]))
            prep["state_decay"].append(
                jnp.where(low[:1], chunk_decay[h0:h0 + 1, :], chunk_decay[h0 + 1:h0 + 2, :]))
    return prep


def _ssd_scan(prep, cb, z, dsk_ref, ng_ref, state_ref):
    y_groups = []
    for g in range(SSM_GROUPS):
        prev = state_ref[g]
        y_off = jnp.dot(prep["cm"][g], prev.astype(BF16), preferred_element_type=F32)
        y_tiles, new_tiles = [], []
        for pr in range(HEADS_PER_GROUP // 2):
            i = g * (HEADS_PER_GROUP // 2) + pr
            m_pair = jnp.concatenate([(cb[g] * d).astype(BF16) for d in prep["decay_dt"][i]], axis=1)
            y_diag = jnp.dot(m_pair, prep["rhs"][i], preferred_element_type=F32)
            new = jnp.dot(prep["b_end"][i], prep["rhs"][i], preferred_element_type=F32)
            ps = slice(pr * LANES, (pr + 1) * LANES)
            y_tiles.append(y_diag + y_off[:, ps] * prep["from_start"][i])
            new_tiles.append(prev[:, ps] * prep["state_decay"][i] + new)
        state_ref[g] = jnp.concatenate(new_tiles, axis=1)
        y_groups.append(jnp.concatenate(y_tiles, axis=1))
    y = jnp.concatenate(y_groups, axis=1) + prep["xs"] * dsk_ref[...]
    zf = z.astype(F32)
    y = y * (zf * jax.nn.sigmoid(zf))
    y = jnp.concatenate([_rms_scale(y[:, g * D_GROUP:(g + 1) * D_GROUP]) for g in range(SSM_GROUPS)], axis=-1)
    return (y * ng_ref[...]).astype(BF16)


def _layer_tail_kernel(tiles_per_seq,
                       q_ref, kvp_ref, kvc_ref, z_ref, xbc_ref, dtr_ref, x_ref,
                       bias_ref, sink_ref, cw_ref, cb_ref, dtb_ref, alog_ref, dsk_ref, ng_ref,
                       wo_ref, g_ref, wu_ref, wd_ref,
                       o_ref, mix_ref, h_ref, u_ref, halo_ref, state_ref):
    s = pl.program_id(0)
    seq_start = s % tiles_per_seq == 0

    @pl.when(s == 0)
    def _():
        mix_ref[1] = jnp.zeros(mix_ref.shape[1:], BF16)

    @pl.when(seq_start)
    def _():
        halo_ref[...] = jnp.zeros(halo_ref.shape, F32)
        state_ref[...] = jnp.zeros(state_ref.shape, F32)

    slot = s % 2
    mix_in = mix_ref.at[1 - slot]
    mix_out = mix_ref.at[slot]

    def out_proj():
        x1 = x_ref[...] + jnp.dot(mix_in[...], wo_ref[...], preferred_element_type=F32)
        o_ref[...] = x1
        h_ref[...] = (_rms_scale(x1) * g_ref[...]).astype(BF16)

    def up_piece(j):
        def run():
            ff = slice(j * FF_TILE, (j + 1) * FF_TILE)
            u = jnp.dot(h_ref[...], wu_ref[:, ff], preferred_element_type=F32).astype(BF16)
            u = jnp.maximum(u, 0)
            u_ref[:, ff] = u * u
        return run

    def down_piece(n):
        def run():
            cols = slice(n * DOWN_TILE, (n + 1) * DOWN_TILE)
            o_ref[:, cols] += jnp.dot(u_ref[...], wd_ref[:, cols], preferred_element_type=F32)
        return run

    dense = [out_proj] + [up_piece(j) for j in range(D_FF // FF_TILE)] \
        + [down_piece(n) for n in range(D_MODEL // DOWN_TILE)]

    held = [dict() for _ in range(SUB_TILES)]

    def rows(c):
        return slice(c * BLOCK, (c + 1) * BLOCK)

    def prepare(c):
        raw = xbc_ref[rows(c), :].astype(F32)
        halo = halo_ref[...] if c == 0 else held[c - 1]["halo"]
        held[c]["halo"] = raw[CHUNK - SUBLANES:, :]
        if c == SUB_TILES - 1:
            halo_ref[...] = held[c]["halo"]
        xbc = _causal_conv_silu(raw, halo, cw_ref, cb_ref)
        held[c]["prep"] = _ssd_prepare(xbc, dtr_ref[:, rows(c)], dtb_ref, alog_ref)

    def scores(c):
        if c == 0:
            kv_prev = kvp_ref[...]
            k_prev, v_t_prev = kv_prev[:, :D_KV], kv_prev[:, D_KV:].astype(F32).T.astype(BF16)
        else:
            k_prev, v_t_prev = held[c - 1]["k"], held[c - 1]["v_t"]
        kv_cur = kvc_ref[rows(c), :]
        held[c]["k"] = kv_cur[:, :D_KV]
        held[c]["v_t"] = kv_cur[:, D_KV:].astype(F32).T.astype(BF16)
        held[c]["v_t_band"] = jnp.concatenate([v_t_prev, held[c]["v_t"]], axis=1)
        k_band = jnp.concatenate([k_prev, held[c]["k"]], axis=0)
        variant = jnp.where(seq_start, 0, 1) if c == 0 else 1
        held[c]["probs"] = _attention_scores(q_ref[rows(c), :], k_band, bias_ref, variant, sink_ref)
        prep = held[c]["prep"]
        held[c]["cb"] = [jnp.dot(prep["cm"][g], prep["bm_t"][g], preferred_element_type=F32)
                         for g in range(SSM_GROUPS)]

    def finish(c):
        mix_out[rows(c), :D_ATTN] = _attention_values(held[c]["probs"], held[c]["v_t_band"])
        mix_out[rows(c), D_ATTN:] = _ssd_scan(held[c]["prep"], held[c]["cb"], z_ref[rows(c), :],
                                              dsk_ref, ng_ref, state_ref)

    program = "p0 O U s0 U f0  p1 U U s1 U f1  p2 U U s2 U f2  p3 D D s3 D f3 D".split()
    stage = {"p": prepare, "s": scores, "f": finish}
    pieces = iter(dense)
    for item in program:
        if item in ("O", "U", "D"):
            next(pieces)()
        else:
            stage[item[0]](int(item[1]))
    assert next(pieces, None) is None


def _layer_tail(x2, q, kv, z, xbc, dt_rows, bias_t, sink_rows, conv_w, conv_b, dt_bias, a_log, d_skip, norm_g,
                w_out, g, w_up, w_down, seq):
    t = x2.shape[0]
    n_tiles = t // TOKEN_TILE
    tiles_per_seq = seq // TOKEN_TILE
    mixer = lambda s: (jnp.minimum(s, n_tiles - 1), 0)
    dense = lambda s: (jnp.maximum(s - 1, 0), 0)

    def prev_block(s):
        tile = jnp.minimum(s, n_tiles - 1)
        return (tile * SUB_TILES - jnp.where(tile % tiles_per_seq == 0, 0, 1), 0)

    return pl.pallas_call(
        functools.partial(_layer_tail_kernel, tiles_per_seq),
        grid=(n_tiles + 1,),
        in_specs=[pl.BlockSpec((TOKEN_TILE, D_ATTN), mixer),
                  pl.BlockSpec((BLOCK, 2 * D_KV), prev_block),
                  pl.BlockSpec((TOKEN_TILE, 2 * D_KV), mixer),
                  pl.BlockSpec((TOKEN_TILE, D_SSM), mixer),
                  pl.BlockSpec((TOKEN_TILE, D_CONV), mixer),
                  pl.BlockSpec((SSM_HEADS, TOKEN_TILE), lambda s: (0, jnp.minimum(s, n_tiles - 1))),
                  pl.BlockSpec((TOKEN_TILE, D_MODEL), dense),
                  _resident((2, N_KV_HEADS, 2 * BLOCK, Q_STACK)),
                  _resident((N_KV_HEADS, 1, Q_STACK)),
                  _resident((CONV_WIDTH, D_CONV)),
                  _resident((1, D_CONV)),
                  _resident((SSM_HEADS, CHUNK)),
                  _resident((SSM_HEADS, CHUNK)),
                  _resident((1, D_SSM)),
                  _resident((1, D_SSM)),
                  _resident((D_MIX, D_MODEL)),
                  _resident((1, D_MODEL)),
                  _resident((D_MODEL, D_FF)),
                  _resident((D_FF, D_MODEL))],
        out_specs=pl.BlockSpec((TOKEN_TILE, D_MODEL), dense),
        out_shape=jax.ShapeDtypeStruct((t, D_MODEL), F32),
        scratch_shapes=[pltpu.VMEM((2, TOKEN_TILE, D_MIX), BF16),
                        pltpu.VMEM((TOKEN_TILE, D_MODEL), BF16),
                        pltpu.VMEM((TOKEN_TILE, D_FF), BF16),
                        pltpu.VMEM((SUBLANES, D_CONV), F32),
                        pltpu.VMEM((SSM_GROUPS, SSM_STATE, D_GROUP), F32)],
        compiler_params=pltpu.CompilerParams(
            dimension_semantics=("arbitrary",), vmem_limit_bytes=VMEM_LIMIT),
        name="layer_tail",
    )(q, kv, kv, z, xbc, dt_rows, x2, bias_t, sink_rows, conv_w, conv_b, dt_bias, a_log, d_skip, norm_g,
      w_out, g, w_up, w_down)


def _head_rows(v):
    return jnp.broadcast_to(v[:, None], (SSM_HEADS, CHUNK))


def kernel(x, mix_norm_g, w_in, q_gain, k_gain, sinks, rel_bias, conv_w, conv_b, dt_bias, a_log,
           d_skip, ssm_norm_g, w_out, mlp_norm_g, w_up, w_down):
    batch, seq, _ = x.shape
    depth = w_in.shape[0]
    assert seq % TOKEN_TILE == 0
    bias_t = _bias_table(rel_bias)
    x2 = x.reshape(batch * seq, D_MODEL)
    for l in range(depth):
        w_pad = jnp.pad(w_in[l], ((0, 0), (0, D_IN_PAD - w_in.shape[2]))).astype(BF16)
        q_gain2 = jnp.tile(q_gain[l] * (HEAD_DIM ** -0.5), 2)[None, :]
        k_gain2 = jnp.tile(k_gain[l], 2)[None, :]
        q, kv, z, xbc, dt_rows = _in_proj(x2, mix_norm_g[l][None, :], w_pad, q_gain2, k_gain2)
        sink_rows = jnp.repeat(sinks[l].reshape(N_KV_HEADS, 1, Q_PER_KV), BLOCK, axis=-1)
        x2 = _layer_tail(x2, q, kv, z, xbc, dt_rows, bias_t, sink_rows, conv_w[l], conv_b[l][None, :],
                         _head_rows(dt_bias[l]), _head_rows(a_log[l]),
                         jnp.repeat(d_skip[l], SSM_HEAD_DIM)[None, :], ssm_norm_g[l][None, :],
                         w_out[l].astype(BF16), mlp_norm_g[l][None, :],
                         w_up[l].astype(BF16), w_down[l].astype(BF16), seq)
    return x2.reshape(batch, seq, D_MODEL)
```

```python
import functools
import math

import numpy as np
import jax
import jax.numpy as jnp
from jax import lax
from jax.experimental import pallas as pl
from jax.experimental.pallas import tpu as pltpu

D_MODEL = 1024
HEAD_DIM = 64
N_Q_HEADS = 8
N_KV_HEADS = 2
Q_PER_KV = N_Q_HEADS // N_KV_HEADS
WINDOW = 128
BLOCK = 128
N_BUCKETS = 32
MAX_DISTANCE = 128
SSM_HEADS = 8
SSM_HEAD_DIM = 64
SSM_GROUPS = 2
HEADS_PER_GROUP = SSM_HEADS // SSM_GROUPS
SSM_STATE = 128
CONV_WIDTH = 4
CHUNK = 128
D_FF = 4 * D_MODEL
D_ATTN = N_Q_HEADS * HEAD_DIM
D_KV = N_KV_HEADS * HEAD_DIM
D_SSM = SSM_HEADS * SSM_HEAD_DIM
D_BC = SSM_GROUPS * SSM_STATE
D_CONV = D_SSM + 2 * D_BC
D_GROUP = D_SSM // SSM_GROUPS
D_MIX = D_ATTN + D_SSM
EPS = 1e-6
assert BLOCK == CHUNK

LANES = 128
SUBLANES = 8
DT_PAD = LANES
COL_Q = 0
COL_KV = D_ATTN
COL_Z = COL_KV + 2 * D_KV
COL_XBC = COL_Z + D_SSM
COL_DT = COL_XBC + D_CONV
D_IN_PAD = COL_DT + DT_PAD
Q_STACK = Q_PER_KV * BLOCK

TOKEN_TILE = 512
SUB_TILES = TOKEN_TILE // BLOCK
FF_TILE = 512
DOWN_TILE = D_MODEL // SUB_TILES
VMEM_LIMIT = 56 * 1024 * 1024

F32 = jnp.float32
BF16 = jnp.bfloat16


def _rms_scale(x):
    return x * lax.rsqrt(jnp.mean(jnp.square(x), axis=-1, keepdims=True) + EPS)


def _resident(shape):
    nd = len(shape)
    return pl.BlockSpec(shape, lambda *_: (0,) * nd, pipeline_mode=pl.Buffered(1))


def _resident_layer(layer, shape):
    nd = len(shape)
    return pl.BlockSpec((None,) + shape, lambda *_: (layer,) + (0,) * nd, pipeline_mode=pl.Buffered(1))


def _band_bucket_t():
    kj = np.arange(2 * BLOCK)[:, None]
    qi = np.arange(BLOCK)[None, :]
    dist = qi + BLOCK - kj
    in_window = (dist >= 0) & (dist < WINDOW)
    d = np.clip(dist, 0, None)
    max_exact = N_BUCKETS // 2
    d_f = np.maximum(d, 1).astype(np.float32)
    large = max_exact + (np.log(d_f / np.float32(max_exact)) / np.float32(math.log(MAX_DISTANCE / max_exact))
                         * np.float32(N_BUCKETS - max_exact)).astype(np.int32)
    large = np.minimum(large, N_BUCKETS - 1)
    bucket = np.where(d < max_exact, d, large)
    return np.where(in_window, bucket, -1).astype(np.int32)


def _bias_table_kernel(rel_ref, bucket_ref, out_ref):
    bucket = bucket_ref[...]
    has_prev = lax.broadcasted_iota(jnp.int32, bucket.shape, 0) >= BLOCK
    for h in range(N_Q_HEADS):
        acc = jnp.full(bucket.shape, -jnp.inf, F32)
        for b in range(N_BUCKETS):
            acc = jnp.where(bucket == b, rel_ref[b, h], acc)
        g, gi = divmod(h, Q_PER_KV)
        out_ref[1, g, :, gi * BLOCK:(gi + 1) * BLOCK] = acc
        out_ref[0, g, :, gi * BLOCK:(gi + 1) * BLOCK] = jnp.where(has_prev, acc, -jnp.inf)


def _bias_table(rel_bias):
    bucket = jnp.asarray(_band_bucket_t())
    return pl.pallas_call(
        _bias_table_kernel,
        out_shape=jax.ShapeDtypeStruct((2, N_KV_HEADS, 2 * BLOCK, Q_STACK), F32),
        in_specs=[pl.BlockSpec(memory_space=pltpu.SMEM),
                  pl.BlockSpec(memory_space=pltpu.VMEM)],
        out_specs=pl.BlockSpec(memory_space=pltpu.VMEM),
        name="bias_table",
    )(rel_bias, bucket)


def _head_rms(x):
    lane = lax.broadcasted_iota(jnp.int32, (x.shape[0], LANES), 1)
    low = lane < HEAD_DIM
    parts = []
    for c in range(x.shape[1] // LANES):
        blk = x[:, c * LANES:(c + 1) * LANES]
        sq = blk * blk
        ms_lo = jnp.sum(jnp.where(low, sq, 0.0), axis=-1, keepdims=True) * (1.0 / HEAD_DIM)
        ms_hi = jnp.sum(jnp.where(low, 0.0, sq), axis=-1, keepdims=True) * (1.0 / HEAD_DIM)
        parts.append(blk * lax.rsqrt(jnp.where(low, ms_lo, ms_hi) + EPS))
    return parts


def _in_proj_kernel(x_ref, g_ref, w_ref, qg_ref, kg_ref, q_ref, kv_ref, z_ref, xbc_ref, dt_ref):
    h = (_rms_scale(x_ref[...]) * g_ref[...]).astype(BF16)

    def project(lo, hi):
        return jnp.dot(h, w_ref[:, lo:hi], preferred_element_type=F32)

    q = [p * qg_ref[...] for p in _head_rms(project(COL_Q, COL_KV))]
    q_ref[...] = jnp.concatenate(q, axis=-1).astype(BF16)
    kv = project(COL_KV, COL_Z)
    k = [p * kg_ref[...] for p in _head_rms(kv[:, :D_KV])]
    kv_ref[...] = jnp.concatenate(k + [kv[:, D_KV:]], axis=-1).astype(BF16)
    dt_ref[...] = project(COL_DT, D_IN_PAD).T[:SSM_HEADS, :]
    z_ref[...] = project(COL_Z, COL_XBC).astype(BF16)
    half = (COL_XBC + COL_DT) // 2
    xbc_ref[:, :half - COL_XBC] = project(COL_XBC, half).astype(BF16)
    xbc_ref[:, half - COL_XBC:] = project(half, COL_DT).astype(BF16)


def _in_proj(x2, g, w_pad, q_gain2, k_gain2, layer):
    t = x2.shape[0]
    row = lambda i: (i, 0)
    widths = (D_ATTN, 2 * D_KV, D_SSM, D_CONV)
    return pl.pallas_call(
        _in_proj_kernel,
        grid=(t // TOKEN_TILE,),
        in_specs=[pl.BlockSpec((TOKEN_TILE, D_MODEL), row),
                  _resident((1, D_MODEL)),
                  _resident_layer(layer, (D_MODEL, D_IN_PAD)),
                  _resident((1, LANES)),
                  _resident((1, LANES))],
        out_specs=[pl.BlockSpec((TOKEN_TILE, w), row) for w in widths]
                  + [pl.BlockSpec((SSM_HEADS, TOKEN_TILE), lambda i: (0, i))],
        out_shape=[jax.ShapeDtypeStruct((t, w), BF16) for w in widths]
                  + [jax.ShapeDtypeStruct((SSM_HEADS, t), F32)],
        compiler_params=pltpu.CompilerParams(
            dimension_semantics=("arbitrary",), vmem_limit_bytes=VMEM_LIMIT),
        name="in_proj",
    )(x2, g, w_pad, q_gain2, k_gain2)


def _attention_scores(q, k_band, bias_ref, variant, sink_ref):
    out = []
    for g in range(N_KV_HEADS):
        k = k_band[:, g * HEAD_DIM:(g + 1) * HEAD_DIM]
        heads = range(g * Q_PER_KV, (g + 1) * Q_PER_KV)
        qs = jnp.concatenate([q[:, h * HEAD_DIM:(h + 1) * HEAD_DIM] for h in heads], axis=0)
        s = lax.dot_general(k, qs, (((1,), (1,)), ((), ())), preferred_element_type=F32)
        s = s + bias_ref[variant, g]
        sink = sink_ref[g]
        m = jnp.maximum(jnp.max(s, axis=0, keepdims=True), sink)
        p = jnp.exp(s - m)
        denom = jnp.sum(p, axis=0, keepdims=True) + jnp.exp(sink - m)
        out.append((p.astype(BF16), denom))
    return out


def _attention_values(probs, v_t_band):
    out_t = []
    for g, (p, denom) in enumerate(probs):
        o = jnp.dot(v_t_band[g * HEAD_DIM:(g + 1) * HEAD_DIM, :], p, preferred_element_type=F32)
        out_t.append(o / denom)
    cols = []
    for h0 in range(0, N_Q_HEADS, 2):
        pair = []
        for h in (h0, h0 + 1):
            g, gi = divmod(h, Q_PER_KV)
            pair.append(out_t[g][:, gi * BLOCK:(gi + 1) * BLOCK])
        cols.append(jnp.concatenate(pair, axis=0).T)
    return jnp.concatenate(cols, axis=-1).astype(BF16)


def _causal_conv_silu(raw, halo, cw_ref, cb_ref):
    sub = lax.broadcasted_iota(jnp.int32, (SUBLANES, D_CONV), 0)
    acc = cb_ref[...] + cw_ref[CONV_WIDTH - 1:CONV_WIDTH, :] * raw
    for j in range(1, CONV_WIDTH):
        rolled = pltpu.roll(raw, j, axis=0)
        first = jnp.where(sub < j, pltpu.roll(halo, j, axis=0), rolled[:SUBLANES])
        shifted = jnp.concatenate([first, rolled[SUBLANES:]], axis=0)
        acc = acc + cw_ref[CONV_WIDTH - 1 - j:CONV_WIDTH - j, :] * shifted
    return acc * jax.nn.sigmoid(acc)


def _lane_cumsum(x):
    lane = lax.broadcasted_iota(jnp.int32, x.shape, 1)
    shift = 1
    while shift < x.shape[1]:
        x = x + jnp.where(lane >= shift, pltpu.roll(x, shift, axis=1), 0.0)
        shift *= 2
    return x


def _ssd_prepare(xbc, dt_raw, dtb_ref, alog_ref):
    low = lax.broadcasted_iota(jnp.int32, (CHUNK, LANES), 1) < SSM_HEAD_DIM
    row = lax.broadcasted_iota(jnp.int32, (CHUNK, CHUNK), 0)
    col = lax.broadcasted_iota(jnp.int32, (CHUNK, CHUNK), 1)
    causal = row >= col

    dt = jax.nn.softplus(dt_raw + dtb_ref[...])
    a_cs = _lane_cumsum(-(dt * jnp.exp(alog_ref[...])))
    a_last = jnp.broadcast_to(a_cs[:, CHUNK - 1:CHUNK], a_cs.shape)
    w_end = dt * jnp.exp(a_last - a_cs)
    chunk_decay = jnp.exp(a_last)

    xs = xbc[:, :D_SSM]
    xs_b = xs.astype(BF16)
    prep = {"xs": xs, "cm": [], "bm_t": [], "rhs": [], "decay_dt": [], "b_end": [], "from_start": [],
            "state_decay": []}
    for g in range(SSM_GROUPS):
        bm = xbc[:, D_SSM + g * SSM_STATE:D_SSM + (g + 1) * SSM_STATE]
        bm_t = bm.T
        prep["bm_t"].append(bm_t.astype(BF16))
        prep["cm"].append(xbc[:, D_SSM + D_BC + g * SSM_STATE:D_SSM + D_BC + (g + 1) * SSM_STATE].astype(BF16))
        for pr in range(HEADS_PER_GROUP // 2):
            h0 = g * HEADS_PER_GROUP + 2 * pr
            xs_t = xs_b[:, (h0 // 2) * LANES:(h0 // 2 + 1) * LANES]
            zero = jnp.zeros_like(xs_t)
            prep["rhs"].append(jnp.concatenate([jnp.where(low, xs_t, zero), jnp.where(low, zero, xs_t)], axis=0))
            decays, b_parts, e_parts = [], [], []
            for h in (h0, h0 + 1):
                a_row = a_cs[h:h + 1, :]
                a_col = jnp.broadcast_to(a_row, (CHUNK, CHUNK)).T
                decays.append(jnp.exp(jnp.where(causal, a_col - a_row, -jnp.inf)) * dt[h:h + 1, :])
                b_parts.append((bm_t * w_end[h:h + 1, :]).astype(BF16))
                e_parts.append(jnp.exp(a_col))
            prep["decay_dt"].append(decays)
            prep["b_end"].append(jnp.concatenate(b_parts, axis=1))
            prep["from_start"].append(jnp.where(low, e_parts[0], e_parts[1]))
            prep["state_decay"].append(
                jnp.where(low[:1], chunk_decay[h0:h0 + 1, :], chunk_decay[h0 + 1:h0 + 2, :]))
    return prep


def _ssd_scan(prep, cb, z, dsk_ref, ng_ref, state_ref):
    y_groups = []
    for g in range(SSM_GROUPS):
        prev = state_ref[g]
        y_off = jnp.dot(prep["cm"][g], prev.astype(BF16), preferred_element_type=F32)
        y_tiles, new_tiles = [], []
        for pr in range(HEADS_PER_GROUP // 2):
            i = g * (HEADS_PER_GROUP // 2) + pr
            m_pair = jnp.concatenate([(cb[g] * d).astype(BF16) for d in prep["decay_dt"][i]], axis=1)
            y_diag = jnp.dot(m_pair, prep["rhs"][i], preferred_element_type=F32)
            new = jnp.dot(prep["b_end"][i], prep["rhs"][i], preferred_element_type=F32)
            ps = slice(pr * LANES, (pr + 1) * LANES)
            y_tiles.append(y_diag + y_off[:, ps] * prep["from_start"][i])
            new_tiles.append(prev[:, ps] * prep["state_decay"][i] + new)
        state_ref[g] = jnp.concatenate(new_tiles, axis=1)
        y_groups.append(jnp.concatenate(y_tiles, axis=1))
    y = jnp.concatenate(y_groups, axis=1) + prep["xs"] * dsk_ref[...]
    zf = z.astype(F32)
    y = y * (zf * jax.nn.sigmoid(zf))
    y = jnp.concatenate([_rms_scale(y[:, g * D_GROUP:(g + 1) * D_GROUP]) for g in range(SSM_GROUPS)], axis=-1)
    return (y * ng_ref[...]).astype(BF16)


def _layer_tail_kernel(tiles_per_seq, n_tiles,
                       q_ref, kvp_ref, kvc_ref, z_ref, xbc_ref, dtr_ref, x_ref,
                       bias_ref, sink_ref, cw_ref, cb_ref, dtb_ref, alog_ref, dsk_ref, ng_ref,
                       wo_ref, g_ref, wu_ref, wd_ref,
                       o_ref, mix_ref, h_ref, u_ref, halo_ref, state_ref):
    s = pl.program_id(0)
    seq_start = s % tiles_per_seq == 0

    @pl.when(seq_start)
    def _():
        halo_ref[...] = jnp.zeros(halo_ref.shape, F32)
        state_ref[...] = jnp.zeros(state_ref.shape, F32)

    slot = s % 2
    mix_in = mix_ref.at[1 - slot]
    mix_out = mix_ref.at[slot]

    def out_proj():
        x1 = x_ref[...] + jnp.dot(mix_in[...], wo_ref[...], preferred_element_type=F32)
        o_ref[...] = x1
        h_ref[...] = (_rms_scale(x1) * g_ref[...]).astype(BF16)

    def up_piece(j):
        def run():
            ff = slice(j * FF_TILE, (j + 1) * FF_TILE)
            u = jnp.dot(h_ref[...], wu_ref[:, ff], preferred_element_type=F32).astype(BF16)
            u = jnp.maximum(u, 0)
            u_ref[:, ff] = u * u
        return run

    def down_piece(n):
        def run():
            cols = slice(n * DOWN_TILE, (n + 1) * DOWN_TILE)
            o_ref[:, cols] += jnp.dot(u_ref[...], wd_ref[:, cols], preferred_element_type=F32)
        return run

    dense = [out_proj] + [up_piece(j) for j in range(D_FF // FF_TILE)] \
        + [down_piece(n) for n in range(D_MODEL // DOWN_TILE)]

    held = [dict() for _ in range(SUB_TILES)]

    def rows(c):
        return slice(c * BLOCK, (c + 1) * BLOCK)

    def prepare(c):
        raw = xbc_ref[rows(c), :].astype(F32)
        halo = halo_ref[...] if c == 0 else held[c - 1]["halo"]
        held[c]["halo"] = raw[CHUNK - SUBLANES:, :]
        if c == SUB_TILES - 1:
            halo_ref[...] = held[c]["halo"]
        xbc = _causal_conv_silu(raw, halo, cw_ref, cb_ref)
        held[c]["prep"] = _ssd_prepare(xbc, dtr_ref[:, rows(c)], dtb_ref, alog_ref)

    def scores(c):
        if c == 0:
            kv_prev = kvp_ref[...]
            k_prev, v_t_prev = kv_prev[:, :D_KV], kv_prev[:, D_KV:].astype(F32).T.astype(BF16)
        else:
            k_prev, v_t_prev = held[c - 1]["k"], held[c - 1]["v_t"]
        kv_cur = kvc_ref[rows(c), :]
        held[c]["k"] = kv_cur[:, :D_KV]
        held[c]["v_t"] = kv_cur[:, D_KV:].astype(F32).T.astype(BF16)
        held[c]["v_t_band"] = jnp.concatenate([v_t_prev, held[c]["v_t"]], axis=1)
        k_band = jnp.concatenate([k_prev, held[c]["k"]], axis=0)
        variant = jnp.where(seq_start, 0, 1) if c == 0 else 1
        held[c]["probs"] = _attention_scores(q_ref[rows(c), :], k_band, bias_ref, variant, sink_ref)
        prep = held[c]["prep"]
        held[c]["cb"] = [jnp.dot(prep["cm"][g], prep["bm_t"][g], preferred_element_type=F32)
                         for g in range(SSM_GROUPS)]

    def finish(c):
        mix_out[rows(c), :D_ATTN] = _attention_values(held[c]["probs"], held[c]["v_t_band"])
        mix_out[rows(c), D_ATTN:] = _ssd_scan(held[c]["prep"], held[c]["cb"], z_ref[rows(c), :],
                                              dsk_ref, ng_ref, state_ref)

    program = "p0 O s0 U f0 U  p1 U s1 U U f1 U  p2 U s2 U f2  p3 D s3 D f3 D D".split()
    stage = {"p": prepare, "s": scores, "f": finish}

    def run(items):
        for d in held:
            d.clear()
        pieces = iter(dense)
        for item in items:
            if item in ("O", "U", "D"):
                next(pieces)()
            else:
                stage[item[0]](int(item[1]))

    pl.when(s == 0)(lambda: run([i for i in program if i[0] in stage]))
    pl.when((s > 0) & (s < n_tiles))(lambda: run(program))
    pl.when(s == n_tiles)(lambda: run([i for i in program if i[0] not in stage]))


def _layer_tail(x2, q, kv, z, xbc, dt_rows, bias_t, sink_rows, conv_w, conv_b, dt_bias, a_log, d_skip, norm_g,
                w_out, g, w_up, w_down, seq, layer):
    t = x2.shape[0]
    n_tiles = t // TOKEN_TILE
    tiles_per_seq = seq // TOKEN_TILE
    mixer = lambda s: (jnp.minimum(s, n_tiles - 1), 0)
    dense = lambda s: (jnp.maximum(s - 1, 0), 0)

    def prev_block(s):
        tile = jnp.minimum(s, n_tiles - 1)
        return (tile * SUB_TILES - jnp.where(tile % tiles_per_seq == 0, 0, 1), 0)

    return pl.pallas_call(
        functools.partial(_layer_tail_kernel, tiles_per_seq, n_tiles),
        grid=(n_tiles + 1,),
        in_specs=[pl.BlockSpec((TOKEN_TILE, D_ATTN), mixer),
                  pl.BlockSpec((BLOCK, 2 * D_KV), prev_block),
                  pl.BlockSpec((TOKEN_TILE, 2 * D_KV), mixer),
                  pl.BlockSpec((TOKEN_TILE, D_SSM), mixer),
                  pl.BlockSpec((TOKEN_TILE, D_CONV), mixer),
                  pl.BlockSpec((SSM_HEADS, TOKEN_TILE), lambda s: (0, jnp.minimum(s, n_tiles - 1))),
                  pl.BlockSpec((TOKEN_TILE, D_MODEL), dense),
                  _resident((2, N_KV_HEADS, 2 * BLOCK, Q_STACK)),
                  _resident((N_KV_HEADS, 1, Q_STACK)),
                  _resident((CONV_WIDTH, D_CONV)),
                  _resident((1, D_CONV)),
                  _resident((SSM_HEADS, CHUNK)),
                  _resident((SSM_HEADS, CHUNK)),
                  _resident((1, D_SSM)),
                  _resident((1, D_SSM)),
                  _resident_layer(layer, (D_MIX, D_MODEL)),
                  _resident((1, D_MODEL)),
                  _resident_layer(layer, (D_MODEL, D_FF)),
                  _resident_layer(layer, (D_FF, D_MODEL))],
        out_specs=pl.BlockSpec((TOKEN_TILE, D_MODEL), dense),
        out_shape=jax.ShapeDtypeStruct((t, D_MODEL), F32),
        scratch_shapes=[pltpu.VMEM((2, TOKEN_TILE, D_MIX), BF16),
                        pltpu.VMEM((TOKEN_TILE, D_MODEL), BF16),
                        pltpu.VMEM((TOKEN_TILE, D_FF), BF16),
                        pltpu.VMEM((SUBLANES, D_CONV), F32),
                        pltpu.VMEM((SSM_GROUPS, SSM_STATE, D_GROUP), F32)],
        compiler_params=pltpu.CompilerParams(
            dimension_semantics=("arbitrary",), vmem_limit_bytes=VMEM_LIMIT),
        name="layer_tail",
    )(q, kv, kv, z, xbc, dt_rows, x2, bias_t, sink_rows, conv_w, conv_b, dt_bias, a_log, d_skip, norm_g,
      w_out, g, w_up, w_down)


def _head_rows(v):
    return jnp.broadcast_to(v[:, None], (SSM_HEADS, CHUNK))


def kernel(x, mix_norm_g, w_in, q_gain, k_gain, sinks, rel_bias, conv_w, conv_b, dt_bias, a_log,
           d_skip, ssm_norm_g, w_out, mlp_norm_g, w_up, w_down):
    batch, seq, _ = x.shape
    depth = w_in.shape[0]
    assert seq % TOKEN_TILE == 0
    bias_t = _bias_table(rel_bias)
    x2 = x.reshape(batch * seq, D_MODEL)
    w_pad = jnp.pad(w_in, ((0, 0), (0, 0), (0, D_IN_PAD - w_in.shape[2]))).astype(BF16)
    w_out_b, w_up_b, w_down_b = w_out.astype(BF16), w_up.astype(BF16), w_down.astype(BF16)
    for l in range(depth):
        q_gain2 = jnp.tile(q_gain[l] * (HEAD_DIM ** -0.5), 2)[None, :]
        k_gain2 = jnp.tile(k_gain[l], 2)[None, :]
        q, kv, z, xbc, dt_rows = _in_proj(x2, mix_norm_g[l][None, :], w_pad, q_gain2, k_gain2, l)
        sink_rows = jnp.repeat(sinks[l].reshape(N_KV_HEADS, 1, Q_PER_KV), BLOCK, axis=-1)
        x2 = _layer_tail(x2, q, kv, z, xbc, dt_rows, bias_t, sink_rows, conv_w[l], conv_b[l][None, :],
                         _head_rows(dt_bias[l]), _head_rows(a_log[l]),
                         jnp.repeat(d_skip[l], SSM_HEAD_DIM)[None, :], ssm_norm_g[l][None, :],
                         w_out_b, mlp_norm_g[l][None, :], w_up_b, w_down_b, seq, l)
    return x2.reshape(batch, seq, D_MODEL)
```
